```python
import jax, jax.numpy as jnp
from jax import lax
import numpy as np

D_MODEL = 2048
BATCH = 4
SEQ = 2048
DEPTH = 2
DEC_BATCH = 128
DEC_SEQ = 8
PAST_LEN = 16384
PAGE_SIZE = 128

N_MIXERS = 2
N_CONV_LAYERS = (DEPTH + 1) // 2
N_GMLP_LAYERS = DEPTH // 2
D_CONV = D_MODEL
CONV_W = 31
D_GMLP = D_MODEL
CHUNK = 128
N_SG = 8
SG_DIM = D_GMLP // N_SG
D_FF = 7 * D_MODEL // 2
N_EXP = 8
TOP_K = 2
EPS_RMS = 1e-6
EPS_LN = 1e-5

kernel_name = "conformer_conv_gmlp_hybrid_step"


def rmsnorm(x, g):
    xf = x.astype(jnp.float32)
    y = xf * lax.rsqrt(jnp.mean(xf * xf, axis=-1, keepdims=True) + EPS_RMS)
    return (y * g.astype(jnp.float32)).astype(x.dtype)


def layernorm(x, g, b):
    xf = x.astype(jnp.float32)
    mu = jnp.mean(xf, axis=-1, keepdims=True)
    xc = xf - mu
    y = xc * lax.rsqrt(jnp.mean(xc * xc, axis=-1, keepdims=True) + EPS_LN)
    return (y * g.astype(jnp.float32) + b.astype(jnp.float32)).astype(x.dtype)


def conv_mixer(h, buf, w_in, b_in, w_dw, b_dw, ln_g, ln_b, w_out, b_out):
    z = h @ w_in + b_in
    a, gate = jnp.split(z, 2, axis=-1)
    glu = a * jax.nn.sigmoid(gate)
    xp = jnp.concatenate([buf.astype(glu.dtype), glu], axis=1)
    y = lax.conv_general_dilated(
        xp, w_dw[:, None, :].astype(xp.dtype), window_strides=(1,), padding="VALID",
        dimension_numbers=("NWC", "WIO", "NWC"), feature_group_count=D_CONV) + b_dw
    y = jax.nn.silu(layernorm(y, ln_g, ln_b))
    return y @ w_out + b_out, xp[:, -(CONV_W - 1):]


def gmlp_mixer(h, w_in, b_in, ln_g, ln_b, w_s, b_s, w_out, b_out):
    z = jax.nn.gelu(h @ w_in + b_in, approximate=False)
    u, v = jnp.split(z, 2, axis=-1)
    v = layernorm(v, ln_g, ln_b)
    B, T, _ = v.shape
    n_chunks = -(-T // CHUNK)
    vp = jnp.pad(v, ((0, 0), (0, n_chunks * CHUNK - T), (0, 0)))
    vc = vp.reshape(B, n_chunks, CHUNK, N_SG, SG_DIM)
    mask = jnp.tril(jnp.ones((CHUNK, CHUNK), dtype=bool))
    ws = jnp.where(mask[None], w_s, 0).astype(v.dtype)
    mixed = jnp.einsum("gts,bcsgd->bctgd", ws, vc) + b_s.T.astype(v.dtype)[None, None, :, :, None]
    mixed = mixed.reshape(B, n_chunks * CHUNK, D_GMLP)[:, :T]
    return (u * mixed) @ w_out + b_out, v


def swiglu(h, wg, wu, wd):
    return (jax.nn.silu(h @ wg) * (h @ wu)) @ wd


def moe_swiglu(h, router, wg, wu, wd):
    logits = h.astype(jnp.float32) @ router.astype(jnp.float32)
    top_v, top_i = lax.top_k(logits, TOP_K)
    gates = jax.nn.softmax(top_v, axis=-1)
    comb = jnp.sum(jax.nn.one_hot(top_i, N_EXP, dtype=jnp.float32) * gates[..., None], axis=-2)
    comb = comb.astype(h.dtype)
    out = jnp.zeros(h.shape[:-1] + (D_MODEL,), h.dtype)
    for e in range(N_EXP):
        out = out + comb[..., e:e + 1] * swiglu(h, wg[e], wu[e], wd[e])
    return out


def trunk(x, conv_bufs, keep_chunk_rows, norm_mix, norm_ffn, norm_final,
          conv_w_in, conv_b_in, conv_w_dw, conv_b_dw, conv_ln_g, conv_ln_b, conv_w_out, conv_b_out,
          gmlp_w_in, gmlp_b_in, gmlp_ln_g, gmlp_ln_b, gmlp_w_s, gmlp_b_s, gmlp_w_out, gmlp_b_out,
          ffn_w_gate, ffn_w_up, ffn_w_down, moe_router, moe_w_gate, moe_w_up, moe_w_down):
    new_bufs, v_rows = [], []
    for i in range(DEPTH):
        j = i // N_MIXERS
        h = rmsnorm(x, norm_mix[i])
        if i % N_MIXERS == 0:
            out, nb = conv_mixer(h, conv_bufs[j], conv_w_in[j], conv_b_in[j], conv_w_dw[j], conv_b_dw[j],
                                 conv_ln_g[j], conv_ln_b[j], conv_w_out[j], conv_b_out[j])
            new_bufs.append(nb)
        else:
            out, v = gmlp_mixer(h, gmlp_w_in[j], gmlp_b_in[j], gmlp_ln_g[j], gmlp_ln_b[j],
                                gmlp_w_s[j], gmlp_b_s[j], gmlp_w_out[j], gmlp_b_out[j])
            if keep_chunk_rows:
                v_rows.append(v)
        x = x + out
        h = rmsnorm(x, norm_ffn[i])
        if i % 2 == 0:
            x = x + swiglu(h, ffn_w_gate[j], ffn_w_up[j], ffn_w_down[j])
        else:
            x = x + moe_swiglu(h, moe_router[j], moe_w_gate[j], moe_w_up[j], moe_w_down[j])
    return rmsnorm(x, norm_final), new_bufs, v_rows


def setup_inputs(seed: int = 0) -> dict:
    key = jax.random.key(seed)
    ks = iter(jax.random.split(key, 40))

    def nrm(shape, scale):
        return jax.random.normal(next(ks), shape, jnp.float32) * scale

    NC, NG, ND, NM = N_CONV_LAYERS, N_GMLP_LAYERS, N_CONV_LAYERS, N_GMLP_LAYERS
    return {
        "x_prompt": nrm((BATCH, SEQ, D_MODEL), 1.0),
        "x_sample": nrm((DEC_BATCH, DEC_SEQ, D_MODEL), 1.0),
        "state_conv": nrm((NC, DEC_BATCH, CONV_W - 1, D_CONV), 0.5),
        "norm_mix": 1.0 + nrm((DEPTH, D_MODEL), 0.02),
        "norm_ffn": 1.0 + nrm((DEPTH, D_MODEL), 0.02),
        "norm_final": 1.0 + nrm((D_MODEL,), 0.02),
        "conv_w_in": nrm((NC, D_MODEL, 2 * D_CONV), D_MODEL ** -0.5),
        "conv_b_in": nrm((NC, 2 * D_CONV), 0.02),
        "conv_w_dw": nrm((NC, CONV_W, D_CONV), CONV_W ** -0.5),
        "conv_b_dw": nrm((NC, D_CONV), 0.02),
        "conv_ln_g": 1.0 + nrm((NC, D_CONV), 0.02),
        "conv_ln_b": nrm((NC, D_CONV), 0.02),
        "conv_w_out": nrm((NC, D_CONV, D_MODEL), D_CONV ** -0.5),
        "conv_b_out": nrm((NC, D_MODEL), 0.02),
        "gmlp_w_in": nrm((NG, D_MODEL, 2 * D_GMLP), D_MODEL ** -0.5),
        "gmlp_b_in": nrm((NG, 2 * D_GMLP), 0.02),
        "gmlp_ln_g": 1.0 + nrm((NG, D_GMLP), 0.02),
        "gmlp_ln_b": nrm((NG, D_GMLP), 0.02),
        "gmlp_w_s": nrm((NG, N_SG, CHUNK, CHUNK), 0.5 * CHUNK ** -0.5),
        "gmlp_b_s": 1.0 + nrm((NG, N_SG, CHUNK), 0.02),
        "gmlp_w_out": nrm((NG, D_GMLP, D_MODEL), D_GMLP ** -0.5),
        "gmlp_b_out": nrm((NG, D_MODEL), 0.02),
        "ffn_w_gate": nrm((ND, D_MODEL, D_FF), D_MODEL ** -0.5),
        "ffn_w_up": nrm((ND, D_MODEL, D_FF), D_MODEL ** -0.5),
        "ffn_w_down": nrm((ND, D_FF, D_MODEL), D_FF ** -0.5),
        "moe_router": nrm((NM, D_MODEL, N_EXP), D_MODEL ** -0.5),
        "moe_w_gate": nrm((NM, N_EXP, D_MODEL, D_FF), D_MODEL ** -0.5),
        "moe_w_up": nrm((NM, N_EXP, D_MODEL, D_FF), D_MODEL ** -0.5),
        "moe_w_down": nrm((NM, N_EXP, D_FF, D_MODEL), D_FF ** -0.5),
    }


def reference(x_prompt, x_sample, state_conv, norm_mix, norm_ffn, norm_final,
              conv_w_in, conv_b_in, conv_w_dw, conv_b_dw, conv_ln_g, conv_ln_b, conv_w_out, conv_b_out,
              gmlp_w_in, gmlp_b_in, gmlp_ln_g, gmlp_ln_b, gmlp_w_s, gmlp_b_s, gmlp_w_out, gmlp_b_out,
              ffn_w_gate, ffn_w_up, ffn_w_down, moe_router, moe_w_gate, moe_w_up, moe_w_down):
    weights = (norm_mix, norm_ffn, norm_final,
               conv_w_in, conv_b_in, conv_w_dw, conv_b_dw, conv_ln_g, conv_ln_b, conv_w_out, conv_b_out,
               gmlp_w_in, gmlp_b_in, gmlp_ln_g, gmlp_ln_b, gmlp_w_s, gmlp_b_s, gmlp_w_out, gmlp_b_out,
               ffn_w_gate, ffn_w_up, ffn_w_down, moe_router, moe_w_gate, moe_w_up, moe_w_down)
    zero_bufs = jnp.zeros((N_CONV_LAYERS, x_prompt.shape[0], CONV_W - 1, D_CONV), x_prompt.dtype)
    y_prompt, bufs_p, _ = trunk(x_prompt, zero_bufs, False, *weights)
    y_sample, bufs_s, v_s = trunk(x_sample, state_conv, True, *weights)
    new_conv_prompt = jnp.stack(bufs_p, axis=0)
    new_conv_sample = jnp.stack(bufs_s, axis=0)
    new_chunk_v_sample = jnp.stack(v_s, axis=0)
    return (y_prompt, y_sample, new_conv_prompt, new_conv_sample, new_chunk_v_sample)
```

```python
import functools

import jax
import jax.numpy as jnp
from jax import lax
from jax.experimental import pallas as pl
from jax.experimental.pallas import tpu as pltpu

F32 = jnp.float32
BF16 = jnp.bfloat16

D_MODEL = 2048
CONV_W = 31
CHUNK = 128
N_SG = 8
SG_DIM = D_MODEL // N_SG
N_EXP = 8
TOP_K = 2
EPS_RMS = 1e-6
EPS_LN = 1e-5

V7X_VMEM_BYTES = 64 * 1024 * 1024
VMEM_LIMIT = V7X_VMEM_BYTES - 8 * 1024 * 1024
LANES = 128
LANE_TILES = D_MODEL // LANES
HALO = 32


def _params(*sem):
    return pltpu.CompilerParams(dimension_semantics=sem, vmem_limit_bytes=VMEM_LIMIT)


def _rms(x, g):
    return x * lax.rsqrt(jnp.mean(x * x, axis=-1, keepdims=True) + EPS_RMS) * g


def _layernorm(x, g, b):
    mu = jnp.mean(x, axis=-1, keepdims=True)
    xc = x - mu
    return xc * lax.rsqrt(jnp.mean(xc * xc, axis=-1, keepdims=True) + EPS_LN) * g + b


def _norm_kernel(x_ref, g_ref, h_ref):
    h_ref[...] = _rms(x_ref[...], g_ref[...]).astype(BF16)


def _add_norm_kernel(x_ref, d_ref, g_ref, xo_ref, h_ref):
    x = x_ref[...] + d_ref[...]
    xo_ref[...] = x
    h_ref[...] = _rms(x, g_ref[...]).astype(BF16)


def rmsnorm_bf16(x, g, delta=None, tr=512):
    t, d = x.shape
    row = pl.BlockSpec((tr, d), lambda i: (i, 0))
    vec = pl.BlockSpec((1, d), lambda i: (0, 0))
    g = g.reshape(1, d)
    if delta is None:
        return pl.pallas_call(
            _norm_kernel, grid=(t // tr,), in_specs=[row, vec], out_specs=row,
            out_shape=jax.ShapeDtypeStruct((t, d), BF16), compiler_params=_params("parallel"),
            name="rmsnorm")(x, g)
    return pl.pallas_call(
        _add_norm_kernel, grid=(t // tr,), in_specs=[row, row, vec], out_specs=[row, row],
        out_shape=[jax.ShapeDtypeStruct((t, d), F32), jax.ShapeDtypeStruct((t, d), BF16)],
        compiler_params=_params("parallel"), name="add_rmsnorm")(x, delta, g)


def _gelu_exact(z):
    return 0.5 * z * (1.0 + lax.erf(z * (0.5 ** 0.5)))


def _mm_glu_kernel(a_ref, wa_ref, wg_ref, ba_ref, bg_ref, o_ref):
    act = a_ref[...]
    a = jnp.dot(act, wa_ref[...].astype(BF16), preferred_element_type=F32) + ba_ref[...]
    gate = jnp.dot(act, wg_ref[...].astype(BF16), preferred_element_type=F32) + bg_ref[...]
    glu = a * jax.nn.sigmoid(gate)
    for j in range(o_ref.shape[0]):
        o_ref[j] = glu[:, j * LANES:(j + 1) * LANES]


def _mm_gelu_kernel(a_ref, w_ref, b_ref, o_ref):
    z = jnp.dot(a_ref[...], w_ref[...].astype(BF16), preferred_element_type=F32) + b_ref[...]
    o_ref[...] = _gelu_exact(z)


def _mm_res_kernel(a_ref, w_ref, b_ref, r_ref, o_ref):
    z = jnp.dot(a_ref[...], w_ref[...].astype(BF16), preferred_element_type=F32) + b_ref[...]
    o_ref[...] = r_ref[...] + z


def matmul_glu(act, w, b, tm=1024, tn=512):
    t, k = act.shape
    n = w.shape[1] // 2
    nb = n // tn
    b = b.reshape(1, 2 * n)
    return pl.pallas_call(
        _mm_glu_kernel, grid=(t // tm, nb),
        in_specs=[pl.BlockSpec((tm, k), lambda i, j: (i, 0)),
                  pl.BlockSpec((k, tn), lambda i, j: (0, j)),
                  pl.BlockSpec((k, tn), lambda i, j: (0, j + nb)),
                  pl.BlockSpec((1, tn), lambda i, j: (0, j)),
                  pl.BlockSpec((1, tn), lambda i, j: (0, j + nb))],
        out_specs=pl.BlockSpec((tn // LANES, tm, LANES), lambda i, j: (j, i, 0)),
        out_shape=jax.ShapeDtypeStruct((n // LANES, t, LANES), F32),
        compiler_params=_params("parallel", "arbitrary"), name="matmul_glu")(act, w, w, b, b)


def matmul_gelu(act, w, b, tm=1024, tn=512):
    t, k = act.shape
    n = w.shape[1]
    return pl.pallas_call(
        _mm_gelu_kernel, grid=(t // tm, n // tn),
        in_specs=[pl.BlockSpec((tm, k), lambda i, j: (i, 0)),
                  pl.BlockSpec((k, tn), lambda i, j: (0, j)),
                  pl.BlockSpec((1, tn), lambda i, j: (0, j))],
        out_specs=pl.BlockSpec((tm, tn), lambda i, j: (i, j)),
        out_shape=jax.ShapeDtypeStruct((t, n), F32),
        compiler_params=_params("parallel", "arbitrary"), name="matmul_gelu")(act, w, b.reshape(1, n))


def matmul_residual(act, w, b, res, tm=1024, tn=512):
    t, k = act.shape
    n = w.shape[1]
    return pl.pallas_call(
        _mm_res_kernel, grid=(t // tm, n // tn),
        in_specs=[pl.BlockSpec((tm, k), lambda i, j: (i, 0)),
                  pl.BlockSpec((k, tn), lambda i, j: (0, j)),
                  pl.BlockSpec((1, tn), lambda i, j: (0, j)),
                  pl.BlockSpec((tm, tn), lambda i, j: (i, j))],
        out_specs=pl.BlockSpec((tm, tn), lambda i, j: (i, j)),
        out_shape=jax.ShapeDtypeStruct((t, n), F32),
        compiler_params=_params("parallel", "arbitrary"), name="matmul_residual")(
            act, w, b.reshape(1, n), res)


def _conv_taps(xp_ref, y_ref, wdw_ref, bdw_ref, l, x0, y0, tt, rows):
    first = HALO - (CONV_W - 1)
    for r0 in range(0, tt, rows):
        acc = jnp.broadcast_to(bdw_ref[l], (rows, LANES))
        for k in range(CONV_W):
            lo = x0 + r0 + first + k
            acc = acc + xp_ref[l, lo:lo + rows, :] * wdw_ref[l, k:k + 1, :]
        y_ref[l, y0 + r0:y0 + r0 + rows, :] = acc


def _ln_silu_store(y_ref, lng_ref, lnb_ref, o_ref):
    n = y_ref.shape[1]
    d = LANE_TILES * LANES
    s1 = y_ref[0]
    for l in range(1, LANE_TILES):
        s1 = s1 + y_ref[l]
    mu = jnp.sum(s1, axis=-1, keepdims=True) * (1.0 / d)
    s2 = jnp.zeros((n, LANES), F32)
    for l in range(LANE_TILES):
        c = y_ref[l] - mu
        s2 = s2 + c * c
    inv = lax.rsqrt(jnp.sum(s2, axis=-1, keepdims=True) * (1.0 / d) + EPS_LN)
    for l in range(LANE_TILES):
        y = (y_ref[l] - mu) * inv * lng_ref[l] + lnb_ref[l]
        o_ref[:, l * LANES:(l + 1) * LANES] = (y * jax.nn.sigmoid(y)).astype(BF16)


def _conv_prompt_kernel(halo_ref, main_ref, wdw_ref, bdw_ref, lng_ref, lnb_ref, o_ref, xp_ref, y_ref, *, tt):
    xp_ref[:, 0:HALO, :] = jnp.where(pl.program_id(1) > 0, halo_ref[...], 0.0)
    xp_ref[:, HALO:HALO + tt, :] = main_ref[...]

    def lane_tile(l, carry):
        _conv_taps(xp_ref, y_ref, wdw_ref, bdw_ref, l, 0, 0, tt, 32)
        return carry

    lax.fori_loop(0, LANE_TILES, lane_tile, 0)
    _ln_silu_store(y_ref, lng_ref, lnb_ref, o_ref)


def _conv_sample_kernel(hist_ref, main_ref, wdw_ref, bdw_ref, lng_ref, lnb_ref, o_ref, xp_ref, y_ref, *, ts, sb):
    span = HALO + ts
    for s in range(sb):
        xp_ref[:, s * span:s * span + HALO, :] = hist_ref[s]
        xp_ref[:, s * span + HALO:(s + 1) * span, :] = main_ref[:, s * ts:(s + 1) * ts, :]

    def lane_tile(l, carry):
        for s in range(sb):
            _conv_taps(xp_ref, y_ref, wdw_ref, bdw_ref, l, s * span, s * ts, ts, ts)
        return carry

    lax.fori_loop(0, LANE_TILES, lane_tile, 0)
    _ln_silu_store(y_ref, lng_ref, lnb_ref, o_ref)


def _lane_tiled(v):
    return v.reshape(v.shape[0], LANE_TILES, LANES).transpose(1, 0, 2)


def conv_ln_silu(glu_t, hist_t, w_dw, b_dw, ln_g, ln_b, n_prompt_seq, prompt_len, n_sample_seq, sample_len, tt=256):
    lt, t, _ = glu_t.shape
    d = lt * LANES
    tp = n_prompt_seq * prompt_len
    z3 = lambda *_: (0, 0, 0)
    wspecs = [pl.BlockSpec((lt, CONV_W, LANES), z3), pl.BlockSpec((lt, 1, LANES), z3),
              pl.BlockSpec((lt, 1, LANES), z3), pl.BlockSpec((lt, 1, LANES), z3)]
    wargs = (_lane_tiled(w_dw), _lane_tiled(b_dw.reshape(1, d)), _lane_tiled(ln_g.reshape(1, d)),
             _lane_tiled(ln_b.reshape(1, d)))
    tiles = prompt_len // tt
    per = tt // HALO
    y_p = pl.pallas_call(
        functools.partial(_conv_prompt_kernel, tt=tt), grid=(n_prompt_seq, tiles),
        in_specs=[pl.BlockSpec((lt, HALO, LANES), lambda b, i: (0, jnp.maximum((b * tiles + i) * per - 1, 0), 0)),
                  pl.BlockSpec((lt, tt, LANES), lambda b, i: (0, b * tiles + i, 0))] + wspecs,
        out_specs=pl.BlockSpec((tt, d), lambda b, i: (b * tiles + i, 0)),
        out_shape=jax.ShapeDtypeStruct((tp, d), BF16),
        scratch_shapes=[pltpu.VMEM((lt, HALO + tt, LANES), F32), pltpu.VMEM((lt, tt, LANES), F32)],
        compiler_params=_params("parallel", "parallel"), name="conv_prompt")(glu_t, glu_t, *wargs)
    ts = sample_len
    sb = 16
    base = tp // (sb * ts)
    y_s = pl.pallas_call(
        functools.partial(_conv_sample_kernel, ts=ts, sb=sb), grid=(n_sample_seq // sb,),
        in_specs=[pl.BlockSpec((sb, lt, HALO, LANES), lambda b: (b, 0, 0, 0)),
                  pl.BlockSpec((lt, sb * ts, LANES), lambda b: (0, base + b, 0))] + wspecs,
        out_specs=pl.BlockSpec((sb * ts, d), lambda b: (b, 0)),
        out_shape=jax.ShapeDtypeStruct((n_sample_seq * ts, d), BF16),
        scratch_shapes=[pltpu.VMEM((lt, sb * (HALO + ts), LANES), F32), pltpu.VMEM((lt, sb * ts, LANES), F32)],
        compiler_params=_params("parallel"), name="conv_sample")(hist_t, glu_t, *wargs)
    return jnp.concatenate([y_p, y_s], axis=0)


def _gmlp_mix_kernel(u_ref, v_ref, w_ref, bias_ref, lng_ref, lnb_ref, y_ref, vn_ref):
    vn = _layernorm(v_ref[...], lng_ref[...], lnb_ref[...])
    vn_ref[...] = vn
    vb = vn.astype(BF16)
    for g in range(N_SG):
        cols = slice(g * SG_DIM, (g + 1) * SG_DIM)
        mixed = jnp.dot(w_ref[0, g], vb[:, cols], preferred_element_type=F32) + bias_ref[0, :, cols]
        y_ref[:, cols] = (u_ref[:, cols] * mixed).astype(BF16)


def gmlp_mix(z, w_mix, bias_mix, ln_g, ln_b, n_prompt_chunks):
    t = z.shape[0]
    d = z.shape[1] // 2
    which = lambda c: (c >= n_prompt_chunks).astype(jnp.int32)
    return pl.pallas_call(
        _gmlp_mix_kernel, grid=(t // CHUNK,),
        in_specs=[pl.BlockSpec((CHUNK, d), lambda c: (c, 0)),
                  pl.BlockSpec((CHUNK, d), lambda c: (c, 1)),
                  pl.BlockSpec((1, N_SG, CHUNK, CHUNK), lambda c: (which(c), 0, 0, 0)),
                  pl.BlockSpec((1, CHUNK, d), lambda c: (which(c), 0, 0)),
                  pl.BlockSpec((1, d), lambda c: (0, 0)),
                  pl.BlockSpec((1, d), lambda c: (0, 0))],
        out_specs=[pl.BlockSpec((CHUNK, d), lambda c: (c, 0)), pl.BlockSpec((CHUNK, d), lambda c: (c, 0))],
        out_shape=[jax.ShapeDtypeStruct((t, d), BF16), jax.ShapeDtypeStruct((t, d), F32)],
        compiler_params=_params("parallel"), name="gmlp_mix")(
            z, z, w_mix, bias_mix, ln_g.reshape(1, d), ln_b.reshape(1, d))


def _ffn_kernel(blk_ref, exp_ref, nch_ref, x_ref, wg_ref, wu_ref, wd_ref, o_ref, wgs, wus, wds, *scratch,
                n_chunks, ch, tiled):
    b = pl.program_id(0)
    f = pl.program_id(1)
    nf = pl.num_programs(1)
    nch = nch_ref[b]
    d = wds.shape[-1]
    if tiled:
        x2d, acc = scratch
    else:
        x2d, acc = x_ref, o_ref

    @pl.when(nch > 0)
    def _():
        wgs[...] = wg_ref[0].astype(BF16)
        wus[...] = wu_ref[0].astype(BF16)
        wds[...] = wd_ref[0].astype(BF16)
        for c in range(n_chunks):
            rows = pl.ds(c * ch, ch)

            @pl.when(c < nch)
            def _():
                if tiled:
                    @pl.when(f == 0)
                    def _():
                        x2d[rows, :] = x_ref[rows].reshape(ch, d)

                xc = x2d[rows, :]
                g = jnp.dot(xc, wgs[...], preferred_element_type=F32)
                u = jnp.dot(xc, wus[...], preferred_element_type=F32)
                a = (g * jax.nn.sigmoid(g) * u).astype(BF16)
                dn = jnp.dot(a, wds[...], preferred_element_type=F32)

                @pl.when(f == 0)
                def _():
                    acc[rows, :] = dn

                @pl.when(f > 0)
                def _():
                    acc[rows, :] += dn

                if tiled:
                    @pl.when(f == nf - 1)
                    def _():
                        o_ref[rows] = acc[rows, :].reshape(ch, d // LANES, LANES)

    @pl.when(f == nf - 1)
    def _():
        for c in range(n_chunks):
            @pl.when(c >= nch)
            def _():
                o_ref[pl.ds(c * ch, ch)] = jnp.zeros((ch,) + o_ref.shape[1:], F32)


def grouped_swiglu(x, wg, wu, wd, blk_idx, blk_exp, blk_nch, mblk, ch, tf):
    tiled = x.ndim == 3
    p = x.shape[0]
    d = wd.shape[-1]
    nb = blk_idx.shape[0]
    nf = wg.shape[-1] // tf
    fsel = lambda b, f, nch: jnp.where(nch[b] > 0, f, nf - 1)
    if tiled:
        x_spec = pl.BlockSpec((mblk,) + x.shape[1:], lambda b, f, blk, ex, nch: (blk[b], 0, 0))
        out_spec = pl.BlockSpec((mblk,) + x.shape[1:], lambda b, f, blk, ex, nch: (b, 0, 0),
                                pipeline_mode=pl.Buffered(1))
        scratch = [pltpu.VMEM((mblk, d), BF16), pltpu.VMEM((mblk, d), F32)]
    else:
        x_spec = pl.BlockSpec((mblk, d), lambda b, f, blk, ex, nch: (blk[b], 0))
        out_spec = pl.BlockSpec((mblk, d), lambda b, f, blk, ex, nch: (b, 0))
        scratch = []
    grid_spec = pltpu.PrefetchScalarGridSpec(
        num_scalar_prefetch=3, grid=(nb, nf),
        in_specs=[x_spec,
                  pl.BlockSpec((1, d, tf), lambda b, f, blk, ex, nch: (ex[b], 0, fsel(b, f, nch))),
                  pl.BlockSpec((1, d, tf), lambda b, f, blk, ex, nch: (ex[b], 0, fsel(b, f, nch))),
                  pl.BlockSpec((1, tf, d), lambda b, f, blk, ex, nch: (ex[b], fsel(b, f, nch), 0))],
        out_specs=out_spec,
        scratch_shapes=[pltpu.VMEM((d, tf), BF16), pltpu.VMEM((d, tf), BF16), pltpu.VMEM((tf, d), BF16)] + scratch)
    return pl.pallas_call(
        functools.partial(_ffn_kernel, n_chunks=mblk // ch, ch=ch, tiled=tiled), grid_spec=grid_spec,
        out_shape=jax.ShapeDtypeStruct(x.shape, F32),
        compiler_params=_params("arbitrary", "arbitrary"),
        name="grouped_swiglu_tiled" if tiled else "grouped_swiglu")(blk_idx, blk_exp, blk_nch, x, wg, wu, wd)


def _norm_route_kernel(x_ref, g_ref, r_ref, h_ref, route_ref):
    h = _rms(x_ref[...], g_ref[...])
    h_ref[...] = h.astype(BF16).reshape(h_ref.shape)
    logits = jnp.dot(h, r_ref[...], preferred_element_type=F32, precision=lax.Precision.HIGHEST)
    lane = lax.broadcasted_iota(jnp.int32, logits.shape, 1).astype(F32)
    neg = jnp.float32(-jnp.inf)
    logits = jnp.where(lane < N_EXP, logits, neg)
    m1 = jnp.max(logits, axis=-1, keepdims=True)
    i1 = jnp.min(jnp.where(logits == m1, lane, float(LANES)), axis=-1, keepdims=True)
    rest = jnp.where(lane == i1, neg, logits)
    m2 = jnp.max(rest, axis=-1, keepdims=True)
    i2 = jnp.min(jnp.where(rest == m2, lane, float(LANES)), axis=-1, keepdims=True)
    e2 = jnp.exp(m2 - m1)
    g1 = 1.0 / (1.0 + e2)
    g2 = e2 / (1.0 + e2)
    route = jnp.where(lane == 0, i1, 0.0)
    route = jnp.where(lane == 1, i2, route)
    route = jnp.where(lane == 2, g1, route)
    route = jnp.where(lane == 3, g2, route)
    route_ref[...] = route


def norm_route(x, g, router, tr=512):
    t, d = x.shape
    r = jnp.zeros((d, LANES), F32).at[:, :N_EXP].set(router)
    return pl.pallas_call(
        _norm_route_kernel, grid=(t // tr,),
        in_specs=[pl.BlockSpec((tr, d), lambda i: (i, 0)), pl.BlockSpec((1, d), lambda i: (0, 0)),
                  pl.BlockSpec((d, LANES), lambda i: (0, 0))],
        out_specs=[pl.BlockSpec((tr, d // LANES, LANES), lambda i: (i, 0, 0)),
                   pl.BlockSpec((tr, LANES), lambda i: (i, 0))],
        out_shape=[jax.ShapeDtypeStruct((t, d // LANES, LANES), BF16), jax.ShapeDtypeStruct((t, LANES), F32)],
        compiler_params=_params("parallel"), name="norm_route")(x, g.reshape(1, d), r)


def _gather_rows_kernel(src_ref, valid_ref, h_ref, zero_ref, o_ref, sem, *, rows):
    c = pl.program_id(0)

    def row_copy(r):
        return pltpu.make_async_copy(h_ref.at[pl.ds(src_ref[c * rows + r], 1)],
                                     o_ref.at[pl.ds(c * rows + r, 1)], sem)

    @pl.when(valid_ref[c] > 0)
    def _():
        def issue(r, carry):
            row_copy(r).start()
            return carry

        def drain(r, carry):
            row_copy(r).wait()
            return carry

        lax.fori_loop(0, rows, issue, 0)
        lax.fori_loop(0, rows, drain, 0)

    @pl.when(valid_ref[c] == 0)
    def _():
        fill = pltpu.make_async_copy(zero_ref, o_ref.at[pl.ds(c * rows, rows)], sem)
        fill.start()
        fill.wait()


def gather_rows(h, src, valid, rows):
    p = src.shape[0]
    grid_spec = pltpu.PrefetchScalarGridSpec(
        num_scalar_prefetch=2, grid=(p // rows,),
        in_specs=[pl.BlockSpec(memory_space=pl.ANY), pl.BlockSpec(memory_space=pl.ANY)],
        out_specs=pl.BlockSpec(memory_space=pl.ANY),
        scratch_shapes=[pltpu.SemaphoreType.DMA(())])
    return pl.pallas_call(
        functools.partial(_gather_rows_kernel, rows=rows), grid_spec=grid_spec,
        out_shape=jax.ShapeDtypeStruct((p,) + h.shape[1:], h.dtype),
        compiler_params=_params("arbitrary"), name="gather_rows")(
            src, valid, h, jnp.zeros((rows,) + h.shape[1:], h.dtype))


def _combine_norm_kernel(pos_ref, x_ref, route_ref, g_ref, o_hbm, out_ref, r1_ref, r2_ref, sem, *, tc):
    i = pl.program_id(0)

    def row_copy(r, k, dst):
        return pltpu.make_async_copy(o_hbm.at[pl.ds(pos_ref[TOP_K * (i * tc + r) + k], 1)], dst.at[pl.ds(r, 1)], sem)

    def issue(r, carry):
        row_copy(r, 0, r1_ref).start()
        row_copy(r, 1, r2_ref).start()
        return carry

    def drain(r, carry):
        row_copy(r, 0, r1_ref).wait()
        row_copy(r, 1, r2_ref).wait()
        return carry

    lax.fori_loop(0, tc, issue, 0)
    lax.fori_loop(0, tc, drain, 0)
    route = route_ref[...]
    g1 = route[:, 2:3]
    g2 = route[:, 3:4]
    d = x_ref.shape[-1]
    y = x_ref[...] + (g1 * r1_ref[...].reshape(tc, d) + g2 * r2_ref[...].reshape(tc, d))
    out_ref[...] = _rms(y, g_ref[...])


def combine_norm(x, route, pos, expert_out, g, tc=256):
    t, d = x.shape
    tile = expert_out.shape[1:]
    grid_spec = pltpu.PrefetchScalarGridSpec(
        num_scalar_prefetch=1, grid=(t // tc,),
        in_specs=[pl.BlockSpec((tc, d), lambda i, pos: (i, 0)),
                  pl.BlockSpec((tc, LANES), lambda i, pos: (i, 0)),
                  pl.BlockSpec((1, d), lambda i, pos: (0, 0)),
                  pl.BlockSpec(memory_space=pl.ANY)],
        out_specs=pl.BlockSpec((tc, d), lambda i, pos: (i, 0)),
        scratch_shapes=[pltpu.VMEM((tc,) + tile, F32), pltpu.VMEM((tc,) + tile, F32), pltpu.SemaphoreType.DMA(())])
    return pl.pallas_call(
        functools.partial(_combine_norm_kernel, tc=tc), grid_spec=grid_spec,
        out_shape=jax.ShapeDtypeStruct((t, d), F32),
        compiler_params=_params("arbitrary"), name="combine_norm")(pos, x, route, g.reshape(1, d), expert_out)


def _dispatch_plan(e_slot, n_exp, mblk, ch, n_blocks):
    n_slots = e_slot.shape[0]
    onehot = (e_slot[:, None] == jnp.arange(n_exp, dtype=jnp.int32)[None, :]).astype(jnp.int32)
    csum = jnp.cumsum(onehot, axis=0)
    rank = jnp.sum(csum * onehot, axis=1) - 1
    counts = csum[-1]
    nblk = (counts + mblk - 1) // mblk
    blk_end = jnp.cumsum(nblk)
    blk_start = blk_end - nblk
    dest = blk_start[e_slot] * mblk + rank
    n_used = blk_end[-1]
    step = jnp.arange(n_blocks, dtype=jnp.int32)
    blk_idx = jnp.minimum(step, n_used - 1)
    blk_exp = jnp.minimum(jnp.sum((blk_idx[:, None] >= blk_end[None, :]).astype(jnp.int32), axis=1), n_exp - 1)
    rows_in = jnp.clip(counts[blk_exp] - (blk_idx - blk_start[blk_exp]) * mblk, 0, mblk)
    blk_nch = jnp.where(step < n_used, (rows_in + ch - 1) // ch, 0).astype(jnp.int32)
    src = jnp.zeros((n_blocks * mblk,), jnp.int32).at[dest].set(jnp.arange(n_slots, dtype=jnp.int32) // TOP_K)
    cpb = mblk // ch
    chunk_valid = (jnp.arange(cpb, dtype=jnp.int32)[None, :] < blk_nch[:, None]).astype(jnp.int32).reshape(-1)
    return dest.astype(jnp.int32), src, blk_idx.astype(jnp.int32), blk_exp.astype(jnp.int32), blk_nch, chunk_valid


def kernel(x_prompt, x_sample, state_conv, norm_mix, norm_ffn, norm_final, conv_w_in, conv_b_in, conv_w_dw, conv_b_dw, conv_ln_g, conv_ln_b, conv_w_out, conv_b_out, gmlp_w_in, gmlp_b_in, gmlp_ln_g, gmlp_ln_b, gmlp_w_s, gmlp_b_s, gmlp_w_out, gmlp_b_out, ffn_w_gate, ffn_w_up, ffn_w_down, moe_router, moe_w_gate, moe_w_up, moe_w_down):
    nb_p, len_p, d = x_prompt.shape
    nb_s, len_s, _ = x_sample.shape
    tp, ts = nb_p * len_p, nb_s * len_s
    t = tp + ts
    hist_rows = CONV_W - 1
    x = jnp.concatenate([x_prompt.reshape(tp, d), x_sample.reshape(ts, d)], axis=0)

    h = rmsnorm_bf16(x, norm_mix[0])
    glu_t = matmul_glu(h, conv_w_in[0], conv_b_in[0])
    hist = jnp.pad(state_conv[0], ((0, 0), (HALO - hist_rows, 0), (0, 0)))
    hist_t = hist.reshape(nb_s, HALO, LANE_TILES, LANES).transpose(0, 2, 1, 3)
    y = conv_ln_silu(glu_t, hist_t, conv_w_dw[0], conv_b_dw[0], conv_ln_g[0], conv_ln_b[0], nb_p, len_p, nb_s, len_s)
    x = matmul_residual(y, conv_w_out[0], conv_b_out[0], x)
    glu_p = glu_t[:, :tp].reshape(LANE_TILES, nb_p, len_p, LANES)[:, :, len_p - hist_rows:]
    new_conv_prompt = glu_p.transpose(1, 2, 0, 3).reshape(nb_p, hist_rows, d)[None]
    glu_s = glu_t[:, tp:].reshape(LANE_TILES, nb_s, len_s, LANES).transpose(1, 2, 0, 3).reshape(nb_s, len_s, d)
    new_conv_sample = jnp.concatenate([state_conv[0], glu_s], axis=1)[:, len_s:][None]

    h = rmsnorm_bf16(x, norm_ffn[0])
    mblk0, ch0 = 1024, 512
    nb0 = t // mblk0
    ffn = grouped_swiglu(h, ffn_w_gate, ffn_w_up, ffn_w_down,
                         jnp.arange(nb0, dtype=jnp.int32), jnp.zeros((nb0,), jnp.int32),
                         jnp.full((nb0,), mblk0 // ch0, jnp.int32), mblk0, ch0, 256)

    x, h = rmsnorm_bf16(x, norm_mix[1], delta=ffn)
    z = matmul_gelu(h, gmlp_w_in[0], gmlp_b_in[0])
    tril = jnp.tril(jnp.ones((CHUNK, CHUNK), dtype=bool))
    w_prompt = jnp.where(tril[None], gmlp_w_s[0], 0)
    seqs = CHUNK // len_s
    w_small = jnp.where(tril[None, :len_s, :len_s], gmlp_w_s[0][:, :len_s, :len_s], 0)
    w_sample = jnp.einsum("ab,gts->gatbs", jnp.eye(seqs, dtype=F32), w_small).reshape(N_SG, CHUNK, CHUNK)
    w_mix = jnp.stack([w_prompt, w_sample]).astype(BF16)
    bias_rows = jnp.stack([gmlp_b_s[0].T, jnp.tile(gmlp_b_s[0].T[:len_s], (seqs, 1))])
    bias_mix = jnp.repeat(bias_rows, SG_DIM, axis=2)
    y, v = gmlp_mix(z, w_mix, bias_mix, gmlp_ln_g[0], gmlp_ln_b[0], tp // CHUNK)
    x = matmul_residual(y, gmlp_w_out[0], gmlp_b_out[0], x)
    new_chunk_v_sample = v[tp:].reshape(nb_s, len_s, d)[None]

    h_t, route = norm_route(x, norm_ffn[1], moe_router[0])
    e_slot = route[:, :TOP_K].astype(jnp.int32).reshape(-1)
    mblk1, ch1 = 1024, 512
    n_blocks = (t * TOP_K) // mblk1 + N_EXP
    dest, src, blk_idx, blk_exp, blk_nch, chunk_valid = _dispatch_plan(e_slot, N_EXP, mblk1, ch1, n_blocks)
    xs = gather_rows(h_t, src, chunk_valid, ch1)
    eo = grouped_swiglu(xs, moe_w_gate[0], moe_w_up[0], moe_w_down[0], blk_idx, blk_exp, blk_nch, mblk1, ch1, 256)
    out = combine_norm(x, route, dest, eo, norm_final)

    y_prompt = out[:tp].reshape(nb_p, len_p, d)
    y_sample = out[tp:].reshape(nb_s, len_s, d)
    return (y_prompt, y_sample, new_conv_prompt, new_conv_sample, new_chunk_v_sample)
```

```python
import functools

import jax
import jax.numpy as jnp
from jax import lax
from jax.experimental import pallas as pl
from jax.experimental.pallas import tpu as pltpu

F32 = jnp.float32
BF16 = jnp.bfloat16

D_MODEL = 2048
CONV_W = 31
CHUNK = 128
N_SG = 8
SG_DIM = D_MODEL // N_SG
N_EXP = 8
TOP_K = 2
EPS_RMS = 1e-6
EPS_LN = 1e-5

V7X_VMEM_BYTES = 64 * 1024 * 1024
VMEM_LIMIT = V7X_VMEM_BYTES - 8 * 1024 * 1024
LANES = 128
LANE_TILES = D_MODEL // LANES
HALO = 32
FFN_TF, FFN_TK, FFN_TN = 256, 1024, 1024


def _params(*sem):
    return pltpu.CompilerParams(dimension_semantics=sem, vmem_limit_bytes=VMEM_LIMIT)


def _rms(x, g):
    return x * lax.rsqrt(jnp.mean(x * x, axis=-1, keepdims=True) + EPS_RMS) * g


def _layernorm(x, g, b):
    mu = jnp.mean(x, axis=-1, keepdims=True)
    xc = x - mu
    return xc * lax.rsqrt(jnp.mean(xc * xc, axis=-1, keepdims=True) + EPS_LN) * g + b


def _norm_kernel(x_ref, g_ref, h_ref):
    h_ref[...] = _rms(x_ref[...], g_ref[...]).astype(BF16)


def _add_norm_kernel(x_ref, d_ref, g_ref, xo_ref, h_ref):
    x = x_ref[...] + d_ref[...]
    xo_ref[...] = x
    h_ref[...] = _rms(x, g_ref[...]).astype(BF16)


def rmsnorm_bf16(x, g, delta=None, tr=512):
    t, d = x.shape
    row = pl.BlockSpec((tr, d), lambda i: (i, 0))
    vec = pl.BlockSpec((1, d), lambda i: (0, 0))
    g = g.reshape(1, d)
    if delta is None:
        return pl.pallas_call(
            _norm_kernel, grid=(t // tr,), in_specs=[row, vec], out_specs=row,
            out_shape=jax.ShapeDtypeStruct((t, d), BF16), compiler_params=_params("parallel"),
            name="rmsnorm")(x, g)
    return pl.pallas_call(
        _add_norm_kernel, grid=(t // tr,), in_specs=[row, row, vec], out_specs=[row, row],
        out_shape=[jax.ShapeDtypeStruct((t, d), F32), jax.ShapeDtypeStruct((t, d), BF16)],
        compiler_params=_params("parallel"), name="add_rmsnorm")(x, delta, g)


def _gelu_exact(z):
    return 0.5 * z * (1.0 + lax.erf(z * (0.5 ** 0.5)))


def _cast_on_first_row_tile(w_ref, ws_ref):
    @pl.when(pl.program_id(1) == 0)
    def _():
        ws_ref[...] = w_ref[...].astype(BF16)


def _mm_glu_kernel(a_ref, wa_ref, wg_ref, ba_ref, bg_ref, o_ref, was_ref, wgs_ref):
    _cast_on_first_row_tile(wa_ref, was_ref)
    _cast_on_first_row_tile(wg_ref, wgs_ref)
    act = a_ref[...]
    a = jnp.dot(act, was_ref[...], preferred_element_type=F32) + ba_ref[...]
    gate = jnp.dot(act, wgs_ref[...], preferred_element_type=F32) + bg_ref[...]
    glu = a * jax.nn.sigmoid(gate)
    for j in range(o_ref.shape[0]):
        o_ref[j] = glu[:, j * LANES:(j + 1) * LANES]


def _mm_gelu_kernel(a_ref, w_ref, b_ref, o_ref, ws_ref):
    _cast_on_first_row_tile(w_ref, ws_ref)
    z = jnp.dot(a_ref[...], ws_ref[...], preferred_element_type=F32) + b_ref[...]
    o_ref[...] = _gelu_exact(z)


def _mm_res_kernel(a_ref, w_ref, b_ref, r_ref, o_ref, ws_ref):
    _cast_on_first_row_tile(w_ref, ws_ref)
    z = jnp.dot(a_ref[...], ws_ref[...], preferred_element_type=F32) + b_ref[...]
    o_ref[...] = r_ref[...] + z


def matmul_glu(act, w, b, tm=1024, tn=512):
    t, k = act.shape
    n = w.shape[1] // 2
    nb = n // tn
    b = b.reshape(1, 2 * n)
    return pl.pallas_call(
        _mm_glu_kernel, grid=(nb, t // tm),
        in_specs=[pl.BlockSpec((tm, k), lambda j, i: (i, 0)),
                  pl.BlockSpec((k, tn), lambda j, i: (0, j)),
                  pl.BlockSpec((k, tn), lambda j, i: (0, j + nb)),
                  pl.BlockSpec((1, tn), lambda j, i: (0, j)),
                  pl.BlockSpec((1, tn), lambda j, i: (0, j + nb))],
        out_specs=pl.BlockSpec((tn // LANES, tm, LANES), lambda j, i: (j, i, 0)),
        out_shape=jax.ShapeDtypeStruct((n // LANES, t, LANES), F32),
        scratch_shapes=[pltpu.VMEM((k, tn), BF16), pltpu.VMEM((k, tn), BF16)],
        compiler_params=_params("parallel", "arbitrary"), name="matmul_glu")(act, w, w, b, b)


def matmul_gelu(act, w, b, tm=1024, tn=1024):
    t, k = act.shape
    n = w.shape[1]
    return pl.pallas_call(
        _mm_gelu_kernel, grid=(n // tn, t // tm),
        in_specs=[pl.BlockSpec((tm, k), lambda j, i: (i, 0)),
                  pl.BlockSpec((k, tn), lambda j, i: (0, j)),
                  pl.BlockSpec((1, tn), lambda j, i: (0, j))],
        out_specs=pl.BlockSpec((tm, tn), lambda j, i: (i, j)),
        out_shape=jax.ShapeDtypeStruct((t, n), F32),
        scratch_shapes=[pltpu.VMEM((k, tn), BF16)],
        compiler_params=_params("parallel", "arbitrary"), name="matmul_gelu")(act, w, b.reshape(1, n))


def matmul_residual(act, w, b, res, tm=1024, tn=1024):
    t, k = act.shape
    n = w.shape[1]
    return pl.pallas_call(
        _mm_res_kernel, grid=(n // tn, t // tm),
        in_specs=[pl.BlockSpec((tm, k), lambda j, i: (i, 0)),
                  pl.BlockSpec((k, tn), lambda j, i: (0, j)),
                  pl.BlockSpec((1, tn), lambda j, i: (0, j)),
                  pl.BlockSpec((tm, tn), lambda j, i: (i, j))],
        out_specs=pl.BlockSpec((tm, tn), lambda j, i: (i, j)),
        out_shape=jax.ShapeDtypeStruct((t, n), F32),
        scratch_shapes=[pltpu.VMEM((k, tn), BF16)],
        compiler_params=_params("parallel", "arbitrary"), name="matmul_residual")(
            act, w, b.reshape(1, n), res)


def _conv_taps(xp_ref, y_ref, wdw_ref, bdw_ref, l, x0, y0, tt, rows):
    first = HALO - (CONV_W - 1)
    for r0 in range(0, tt, rows):
        acc = jnp.broadcast_to(bdw_ref[l], (rows, LANES))
        for k in range(CONV_W):
            lo = x0 + r0 + first + k
            acc = acc + xp_ref[l, lo:lo + rows, :] * wdw_ref[l, k:k + 1, :]
        y_ref[l, y0 + r0:y0 + r0 + rows, :] = acc


def _ln_silu_store(y_ref, lng_ref, lnb_ref, o_ref):
    n = y_ref.shape[1]
    d = LANE_TILES * LANES
    s1 = y_ref[0]
    for l in range(1, LANE_TILES):
        s1 = s1 + y_ref[l]
    mu = jnp.sum(s1, axis=-1, keepdims=True) * (1.0 / d)
    s2 = jnp.zeros((n, LANES), F32)
    for l in range(LANE_TILES):
        c = y_ref[l] - mu
        s2 = s2 + c * c
    inv = lax.rsqrt(jnp.sum(s2, axis=-1, keepdims=True) * (1.0 / d) + EPS_LN)
    for l in range(LANE_TILES):
        y = (y_ref[l] - mu) * inv * lng_ref[l] + lnb_ref[l]
        o_ref[:, l * LANES:(l + 1) * LANES] = (y * jax.nn.sigmoid(y)).astype(BF16)


def _conv_prompt_kernel(halo_ref, main_ref, wdw_ref, bdw_ref, lng_ref, lnb_ref, o_ref, xp_ref, y_ref, *, tt):
    xp_ref[:, 0:HALO, :] = jnp.where(pl.program_id(1) > 0, halo_ref[...], 0.0)
    xp_ref[:, HALO:HALO + tt, :] = main_ref[...]

    def lane_tile(l, carry):
        _conv_taps(xp_ref, y_ref, wdw_ref, bdw_ref, l, 0, 0, tt, 32)
        return carry

    lax.fori_loop(0, LANE_TILES, lane_tile, 0)
    _ln_silu_store(y_ref, lng_ref, lnb_ref, o_ref)


def _conv_sample_kernel(hist_ref, main_ref, wdw_ref, bdw_ref, lng_ref, lnb_ref, o_ref, xp_ref, y_ref, *, ts, sb):
    span = HALO + ts
    for s in range(sb):
        xp_ref[:, s * span:s * span + HALO, :] = hist_ref[s]
        xp_ref[:, s * span + HALO:(s + 1) * span, :] = main_ref[:, s * ts:(s + 1) * ts, :]

    def lane_tile(l, carry):
        for s in range(sb):
            _conv_taps(xp_ref, y_ref, wdw_ref, bdw_ref, l, s * span, s * ts, ts, ts)
        return carry

    lax.fori_loop(0, LANE_TILES, lane_tile, 0)
    _ln_silu_store(y_ref, lng_ref, lnb_ref, o_ref)


def _lane_tiled(v):
    return v.reshape(v.shape[0], LANE_TILES, LANES).transpose(1, 0, 2)


def conv_ln_silu(glu_t, hist_t, w_dw, b_dw, ln_g, ln_b, n_prompt_seq, prompt_len, n_sample_seq, sample_len, tt=256):
    lt, t, _ = glu_t.shape
    d = lt * LANES
    tp = n_prompt_seq * prompt_len
    z3 = lambda *_: (0, 0, 0)
    wspecs = [pl.BlockSpec((lt, CONV_W, LANES), z3), pl.BlockSpec((lt, 1, LANES), z3),
              pl.BlockSpec((lt, 1, LANES), z3), pl.BlockSpec((lt, 1, LANES), z3)]
    wargs = (_lane_tiled(w_dw), _lane_tiled(b_dw.reshape(1, d)), _lane_tiled(ln_g.reshape(1, d)),
             _lane_tiled(ln_b.reshape(1, d)))
    tiles = prompt_len // tt
    per = tt // HALO
    y_p = pl.pallas_call(
        functools.partial(_conv_prompt_kernel, tt=tt), grid=(n_prompt_seq, tiles),
        in_specs=[pl.BlockSpec((lt, HALO, LANES), lambda b, i: (0, jnp.maximum((b * tiles + i) * per - 1, 0), 0)),
                  pl.BlockSpec((lt, tt, LANES), lambda b, i: (0, b * tiles + i, 0))] + wspecs,
        out_specs=pl.BlockSpec((tt, d), lambda b, i: (b * tiles + i, 0)),
        out_shape=jax.ShapeDtypeStruct((tp, d), BF16),
        scratch_shapes=[pltpu.VMEM((lt, HALO + tt, LANES), F32), pltpu.VMEM((lt, tt, LANES), F32)],
        compiler_params=_params("parallel", "parallel"), name="conv_prompt")(glu_t, glu_t, *wargs)
    ts = sample_len
    sb = 16
    base = tp // (sb * ts)
    y_s = pl.pallas_call(
        functools.partial(_conv_sample_kernel, ts=ts, sb=sb), grid=(n_sample_seq // sb,),
        in_specs=[pl.BlockSpec((sb, lt, HALO, LANES), lambda b: (b, 0, 0, 0)),
                  pl.BlockSpec((lt, sb * ts, LANES), lambda b: (0, base + b, 0))] + wspecs,
        out_specs=pl.BlockSpec((sb * ts, d), lambda b: (b, 0)),
        out_shape=jax.ShapeDtypeStruct((n_sample_seq * ts, d), BF16),
        scratch_shapes=[pltpu.VMEM((lt, sb * (HALO + ts), LANES), F32), pltpu.VMEM((lt, sb * ts, LANES), F32)],
        compiler_params=_params("parallel"), name="conv_sample")(hist_t, glu_t, *wargs)
    return jnp.concatenate([y_p, y_s], axis=0)


def _gmlp_mix_kernel(u_ref, v_ref, w_ref, bias_ref, lng_ref, lnb_ref, y_ref, vn_ref, *, n_prompt_chunks):
    vn = _layernorm(v_ref[...], lng_ref[...], lnb_ref[...])

    @pl.when(pl.program_id(0) >= n_prompt_chunks)
    def _():
        vn_ref[...] = vn

    vb = vn.astype(BF16)
    for g in range(N_SG):
        cols = slice(g * SG_DIM, (g + 1) * SG_DIM)
        mixed = jnp.dot(w_ref[0, g], vb[:, cols], preferred_element_type=F32) + bias_ref[0, :, cols]
        y_ref[:, cols] = (u_ref[:, cols] * mixed).astype(BF16)


def gmlp_mix(z, w_mix, bias_mix, ln_g, ln_b, n_prompt_chunks):
    t = z.shape[0]
    d = z.shape[1] // 2
    which = lambda c: (c >= n_prompt_chunks).astype(jnp.int32)
    return pl.pallas_call(
        functools.partial(_gmlp_mix_kernel, n_prompt_chunks=n_prompt_chunks), grid=(t // CHUNK,),
        in_specs=[pl.BlockSpec((CHUNK, d), lambda c: (c, 0)),
                  pl.BlockSpec((CHUNK, d), lambda c: (c, 1)),
                  pl.BlockSpec((1, N_SG, CHUNK, CHUNK), lambda c: (which(c), 0, 0, 0)),
                  pl.BlockSpec((1, CHUNK, d), lambda c: (which(c), 0, 0)),
                  pl.BlockSpec((1, d), lambda c: (0, 0)),
                  pl.BlockSpec((1, d), lambda c: (0, 0))],
        out_specs=[pl.BlockSpec((CHUNK, d), lambda c: (c, 0)),
                   pl.BlockSpec((CHUNK, d), lambda c: (jnp.maximum(c - n_prompt_chunks, 0), 0))],
        out_shape=[jax.ShapeDtypeStruct((t, d), BF16), jax.ShapeDtypeStruct((t - n_prompt_chunks * CHUNK, d), F32)],
        compiler_params=_params("arbitrary"), name="gmlp_mix")(
            z, z, w_mix, bias_mix, ln_g.reshape(1, d), ln_b.reshape(1, d))


def _ffn_kernel(blk_ref, exp_ref, nch_ref, x_ref, wg_ref, wu_ref, wd_ref, o_ref, a_ref, acc_ref, *,
                n_chunks, ch, nf, nk, tiled):
    b = pl.program_id(0)
    s = pl.program_id(1)
    nch = nch_ref[b]
    tf = wg_ref.shape[-1]
    tk = wd_ref.shape[1]
    kk = lax.rem(jnp.maximum(s - nf, 0), nk)

    def store_out(rows, val):
        o_ref[rows] = val.reshape((rows.size,) + o_ref.shape[1:]) if tiled else val

    def up_gate(rows):
        xr = x_ref[rows, :]
        g = jnp.dot(xr, wg_ref[0].astype(BF16), preferred_element_type=F32)
        u = jnp.dot(xr, wu_ref[0].astype(BF16), preferred_element_type=F32)
        a_ref[rows, pl.ds(pl.multiple_of(s * tf, tf), tf)] = (g * jax.nn.sigmoid(g) * u).astype(BF16)

    def down(rows):
        slab = pl.ds(pl.multiple_of(kk * tk, tk), tk)
        part = jnp.dot(a_ref[rows, slab], wd_ref[0].astype(BF16), preferred_element_type=F32)
        acc_ref[rows, :] = part + jnp.where(kk > 0, acc_ref[rows, :], 0.0)

        @pl.when(kk == nk - 1)
        def _():
            store_out(rows, acc_ref[rows, :])

    @pl.when(jnp.logical_and(b == 0, s == 0))
    def _():
        acc_ref[...] = jnp.zeros(acc_ref.shape, F32)

    def run(step_fn):
        @pl.when(nch == n_chunks)
        def _():
            step_fn(pl.ds(0, n_chunks * ch))

        @pl.when(jnp.logical_and(nch > 0, nch < n_chunks))
        def _():
            for c in range(n_chunks - 1):
                @pl.when(c < nch)
                def _():
                    step_fn(pl.ds(c * ch, ch))

    @pl.when(s < nf)
    def _():
        run(up_gate)

    @pl.when(s >= nf)
    def _():
        run(down)

    @pl.when(jnp.logical_and(s >= nf, kk == nk - 1))
    def _():
        for c in range(n_chunks):
            @pl.when(c >= nch)
            def _():
                store_out(pl.ds(c * ch, ch), jnp.zeros((ch, acc_ref.shape[-1]), F32))


def grouped_swiglu(x, wg, wu, wd, blk_idx, blk_exp, blk_nch, mblk, ch, tf, tk, tn, tiled):
    p, d = x.shape
    dff = wg.shape[-1]
    nb = blk_idx.shape[0]
    nf, nk, nn = dff // tf, dff // tk, d // tn
    up_tile = lambda b, s, nch: jnp.where(nch[b] > 0, jnp.minimum(s, nf - 1), nf - 1)
    down_step = lambda b, s, nch: jnp.where(nch[b] > 0, jnp.maximum(s - nf, 0), nn * nk - 1)
    out_tile = lambda s: jnp.maximum(s - nf, 0) // nk
    once = dict(pipeline_mode=pl.Buffered(1))
    x_spec = pl.BlockSpec((mblk, d), lambda b, s, blk, ex, nch: (blk[b], 0), **once)
    if tiled:
        out_spec = pl.BlockSpec((mblk, tn // LANES, LANES), lambda b, s, blk, ex, nch: (b, out_tile(s), 0), **once)
        out_shape = jax.ShapeDtypeStruct((p, d // LANES, LANES), F32)
    else:
        out_spec = pl.BlockSpec((mblk, tn), lambda b, s, blk, ex, nch: (b, out_tile(s)), **once)
        out_shape = jax.ShapeDtypeStruct((p, d), F32)
    grid_spec = pltpu.PrefetchScalarGridSpec(
        num_scalar_prefetch=3, grid=(nb, nf + nn * nk),
        in_specs=[x_spec,
                  pl.BlockSpec((1, d, tf), lambda b, s, blk, ex, nch: (ex[b], 0, up_tile(b, s, nch))),
                  pl.BlockSpec((1, d, tf), lambda b, s, blk, ex, nch: (ex[b], 0, up_tile(b, s, nch))),
                  pl.BlockSpec((1, tk, tn), lambda b, s, blk, ex, nch: (
                      ex[b], lax.rem(down_step(b, s, nch), nk), down_step(b, s, nch) // nk))],
        out_specs=out_spec,
        scratch_shapes=[pltpu.VMEM((mblk, dff), BF16), pltpu.VMEM((mblk, tn), F32)])
    return pl.pallas_call(
        functools.partial(_ffn_kernel, n_chunks=mblk // ch, ch=ch, nf=nf, nk=nk, tiled=tiled), grid_spec=grid_spec,
        out_shape=out_shape, compiler_params=_params("arbitrary", "arbitrary"),
        name="grouped_swiglu_tiled" if tiled else "grouped_swiglu")(blk_idx, blk_exp, blk_nch, x, wg, wu, wd)


def _norm_route_kernel(x_ref, g_ref, r_ref, h_ref, route_ref):
    h = _rms(x_ref[...], g_ref[...])
    h_ref[...] = h.astype(BF16).reshape(h_ref.shape)
    logits = jnp.dot(h, r_ref[...], preferred_element_type=F32, precision=lax.Precision.HIGHEST)
    lane = lax.broadcasted_iota(jnp.int32, logits.shape, 1).astype(F32)
    neg = jnp.float32(-jnp.inf)
    logits = jnp.where(lane < N_EXP, logits, neg)
    m1 = jnp.max(logits, axis=-1, keepdims=True)
    i1 = jnp.min(jnp.where(logits == m1, lane, float(LANES)), axis=-1, keepdims=True)
    rest = jnp.where(lane == i1, neg, logits)
    m2 = jnp.max(rest, axis=-1, keepdims=True)
    i2 = jnp.min(jnp.where(rest == m2, lane, float(LANES)), axis=-1, keepdims=True)
    e2 = jnp.exp(m2 - m1)
    g1 = 1.0 / (1.0 + e2)
    g2 = e2 / (1.0 + e2)
    route = jnp.where(lane == 0, i1, 0.0)
    route = jnp.where(lane == 1, i2, route)
    route = jnp.where(lane == 2, g1, route)
    route = jnp.where(lane == 3, g2, route)
    route_ref[...] = route


def norm_route(x, g, router, tr=512):
    t, d = x.shape
    r = jnp.zeros((d, LANES), F32).at[:, :N_EXP].set(router)
    return pl.pallas_call(
        _norm_route_kernel, grid=(t // tr,),
        in_specs=[pl.BlockSpec((tr, d), lambda i: (i, 0)), pl.BlockSpec((1, d), lambda i: (0, 0)),
                  pl.BlockSpec((d, LANES), lambda i: (0, 0))],
        out_specs=[pl.BlockSpec((tr, d // LANES, LANES), lambda i: (i, 0, 0)),
                   pl.BlockSpec((tr, LANES), lambda i: (i, 0))],
        out_shape=[jax.ShapeDtypeStruct((t, d // LANES, LANES), BF16), jax.ShapeDtypeStruct((t, LANES), F32)],
        compiler_params=_params("parallel"), name="norm_route")(x, g.reshape(1, d), r)


def _gather_rows_kernel(src_ref, valid_ref, h_ref, o_ref, buf_ref, sem, *, rows):
    c = pl.program_id(0)
    last = pl.num_programs(0) - 1
    slot = lax.rem(c, 2)

    def for_rows(j, slot_j, action):
        def body(r, carry):
            action(pltpu.make_async_copy(h_ref.at[pl.ds(src_ref[j * rows + r], 1)],
                                         buf_ref.at[slot_j, pl.ds(r, 1)], sem.at[slot_j]))
            return carry

        lax.fori_loop(0, rows, body, 0)

    @pl.when(jnp.logical_and(c == 0, valid_ref[0] > 0))
    def _():
        for_rows(0, 0, lambda cp: cp.start())

    nxt = jnp.minimum(c + 1, last)

    @pl.when(jnp.logical_and(c < last, valid_ref[nxt] > 0))
    def _():
        for_rows(nxt, 1 - slot, lambda cp: cp.start())

    @pl.when(valid_ref[c] > 0)
    def _():
        for_rows(c, slot, lambda cp: cp.wait())
        o_ref[...] = buf_ref[slot].reshape(o_ref.shape)

    @pl.when(valid_ref[c] == 0)
    def _():
        o_ref[...] = jnp.zeros(o_ref.shape, o_ref.dtype)


def gather_rows(h, src, valid, rows):
    p = src.shape[0]
    d = h.shape[1] * h.shape[2]
    grid_spec = pltpu.PrefetchScalarGridSpec(
        num_scalar_prefetch=2, grid=(p // rows,),
        in_specs=[pl.BlockSpec(memory_space=pl.ANY)],
        out_specs=pl.BlockSpec((rows, d), lambda c, src, valid: (c, 0)),
        scratch_shapes=[pltpu.VMEM((2, rows) + h.shape[1:], h.dtype), pltpu.SemaphoreType.DMA((2,))])
    return pl.pallas_call(
        functools.partial(_gather_rows_kernel, rows=rows), grid_spec=grid_spec,
        out_shape=jax.ShapeDtypeStruct((p, d), h.dtype),
        compiler_params=_params("arbitrary"), name="gather_rows")(src, valid, h)


def _combine_norm_kernel(pos_ref, x_ref, route_ref, g_ref, o_hbm, outp_ref, outs_ref, r_ref, sem, *, tc, n_first):
    i = pl.program_id(0)
    last = pl.num_programs(0) - 1
    slot = lax.rem(i, 2)

    def for_rows(j, slot_j, action):
        def body(r, carry):
            for k in range(TOP_K):
                action(pltpu.make_async_copy(o_hbm.at[pl.ds(pos_ref[TOP_K * (j * tc + r) + k], 1)],
                                             r_ref.at[slot_j, k, pl.ds(r, 1)], sem.at[slot_j]))
            return carry

        lax.fori_loop(0, tc, body, 0)

    @pl.when(i == 0)
    def _():
        for_rows(0, 0, lambda cp: cp.start())

    @pl.when(i < last)
    def _():
        for_rows(i + 1, 1 - slot, lambda cp: cp.start())

    for_rows(i, slot, lambda cp: cp.wait())
    route = route_ref[...]
    g1 = route[:, 2:3]
    g2 = route[:, 3:4]
    d = x_ref.shape[-1]
    y = x_ref[...] + (g1 * r_ref[slot, 0].reshape(tc, d) + g2 * r_ref[slot, 1].reshape(tc, d))
    res = _rms(y, g_ref[...])

    @pl.when(i < n_first)
    def _():
        outp_ref[...] = res

    @pl.when(i >= n_first)
    def _():
        outs_ref[...] = res


def combine_norm(x, route, pos, expert_out, g, t_first, tc=256):
    t, d = x.shape
    tile = expert_out.shape[1:]
    n_first = t_first // tc
    grid_spec = pltpu.PrefetchScalarGridSpec(
        num_scalar_prefetch=1, grid=(t // tc,),
        in_specs=[pl.BlockSpec((tc, d), lambda i, pos: (i, 0)),
                  pl.BlockSpec((tc, LANES), lambda i, pos: (i, 0)),
                  pl.BlockSpec((1, d), lambda i, pos: (0, 0)),
                  pl.BlockSpec(memory_space=pl.ANY)],
        out_specs=[pl.BlockSpec((tc, d), lambda i, pos: (jnp.minimum(i, n_first - 1), 0)),
                   pl.BlockSpec((tc, d), lambda i, pos: (jnp.maximum(i - n_first, 0), 0))],
        scratch_shapes=[pltpu.VMEM((2, TOP_K, tc) + tile, F32), pltpu.SemaphoreType.DMA((2,))])
    return pl.pallas_call(
        functools.partial(_combine_norm_kernel, tc=tc, n_first=n_first), grid_spec=grid_spec,
        out_shape=[jax.ShapeDtypeStruct((t_first, d), F32), jax.ShapeDtypeStruct((t - t_first, d), F32)],
        compiler_params=_params("arbitrary"), name="combine_norm")(pos, x, route, g.reshape(1, d), expert_out)


def _dispatch_plan(e_slot, n_exp, mblk, ch, n_blocks):
    n_slots = e_slot.shape[0]
    onehot = (e_slot[:, None] == jnp.arange(n_exp, dtype=jnp.int32)[None, :]).astype(jnp.int32)
    csum = jnp.cumsum(onehot, axis=0)
    rank = jnp.sum(csum * onehot, axis=1) - 1
    counts = csum[-1]
    nblk = (counts + mblk - 1) // mblk
    blk_end = jnp.cumsum(nblk)
    blk_start = blk_end - nblk
    dest = blk_start[e_slot] * mblk + rank
    n_used = blk_end[-1]
    step = jnp.arange(n_blocks, dtype=jnp.int32)
    blk_idx = jnp.minimum(step, n_used - 1)
    blk_exp = jnp.minimum(jnp.sum((blk_idx[:, None] >= blk_end[None, :]).astype(jnp.int32), axis=1), n_exp - 1)
    rows_in = jnp.clip(counts[blk_exp] - (blk_idx - blk_start[blk_exp]) * mblk, 0, mblk)
    blk_nch = jnp.where(step < n_used, (rows_in + ch - 1) // ch, 0).astype(jnp.int32)
    src = jnp.zeros((n_blocks * mblk,), jnp.int32).at[dest].set(jnp.arange(n_slots, dtype=jnp.int32) // TOP_K)
    cpb = mblk // ch
    chunk_valid = (jnp.arange(cpb, dtype=jnp.int32)[None, :] < blk_nch[:, None]).astype(jnp.int32).reshape(-1)
    return dest.astype(jnp.int32), src, blk_idx.astype(jnp.int32), blk_exp.astype(jnp.int32), blk_nch, chunk_valid


def kernel(x_prompt, x_sample, state_conv, norm_mix, norm_ffn, norm_final, conv_w_in, conv_b_in, conv_w_dw, conv_b_dw, conv_ln_g, conv_ln_b, conv_w_out, conv_b_out, gmlp_w_in, gmlp_b_in, gmlp_ln_g, gmlp_ln_b, gmlp_w_s, gmlp_b_s, gmlp_w_out, gmlp_b_out, ffn_w_gate, ffn_w_up, ffn_w_down, moe_router, moe_w_gate, moe_w_up, moe_w_down):
    nb_p, len_p, d = x_prompt.shape
    nb_s, len_s, _ = x_sample.shape
    tp, ts = nb_p * len_p, nb_s * len_s
    t = tp + ts
    hist_rows = CONV_W - 1
    x = jnp.concatenate([x_prompt.reshape(tp, d), x_sample.reshape(ts, d)], axis=0)

    h = rmsnorm_bf16(x, norm_mix[0])
    glu_t = matmul_glu(h, conv_w_in[0], conv_b_in[0])
    hist = jnp.pad(state_conv[0], ((0, 0), (HALO - hist_rows, 0), (0, 0)))
    hist_t = hist.reshape(nb_s, HALO, LANE_TILES, LANES).transpose(0, 2, 1, 3)
    y = conv_ln_silu(glu_t, hist_t, conv_w_dw[0], conv_b_dw[0], conv_ln_g[0], conv_ln_b[0], nb_p, len_p, nb_s, len_s)
    x = matmul_residual(y, conv_w_out[0], conv_b_out[0], x)
    glu_p = glu_t[:, :tp].reshape(LANE_TILES, nb_p, len_p, LANES)[:, :, len_p - hist_rows:]
    new_conv_prompt = glu_p.transpose(1, 2, 0, 3).reshape(nb_p, hist_rows, d)[None]
    glu_s = glu_t[:, tp:].reshape(LANE_TILES, nb_s, len_s, LANES).transpose(1, 2, 0, 3).reshape(nb_s, len_s, d)
    new_conv_sample = jnp.concatenate([state_conv[0], glu_s], axis=1)[:, len_s:][None]

    h = rmsnorm_bf16(x, norm_ffn[0])
    mblk0, ch0 = 1024, 512
    nb0 = t // mblk0
    ffn = grouped_swiglu(h, ffn_w_gate, ffn_w_up, ffn_w_down,
                         jnp.arange(nb0, dtype=jnp.int32), jnp.zeros((nb0,), jnp.int32),
                         jnp.full((nb0,), mblk0 // ch0, jnp.int32), mblk0, ch0, FFN_TF, FFN_TK, FFN_TN, False)

    x, h = rmsnorm_bf16(x, norm_mix[1], delta=ffn)
    z = matmul_gelu(h, gmlp_w_in[0], gmlp_b_in[0])
    tril = jnp.tril(jnp.ones((CHUNK, CHUNK), dtype=bool))
    w_prompt = jnp.where(tril[None], gmlp_w_s[0], 0)
    seqs = CHUNK // len_s
    w_small = jnp.where(tril[None, :len_s, :len_s], gmlp_w_s[0][:, :len_s, :len_s], 0)
    w_sample = jnp.einsum("ab,gts->gatbs", jnp.eye(seqs, dtype=F32), w_small).reshape(N_SG, CHUNK, CHUNK)
    w_mix = jnp.stack([w_prompt, w_sample]).astype(BF16)
    bias_rows = jnp.stack([gmlp_b_s[0].T, jnp.tile(gmlp_b_s[0].T[:len_s], (seqs, 1))])
    bias_mix = jnp.repeat(bias_rows, SG_DIM, axis=2)
    y, v = gmlp_mix(z, w_mix, bias_mix, gmlp_ln_g[0], gmlp_ln_b[0], tp // CHUNK)
    x = matmul_residual(y, gmlp_w_out[0], gmlp_b_out[0], x)
    new_chunk_v_sample = v.reshape(nb_s, len_s, d)[None]

    h_t, route = norm_route(x, norm_ffn[1], moe_router[0])
    e_slot = route[:, :TOP_K].astype(jnp.int32).reshape(-1)
    mblk1, ch1 = 1024, 512
    n_blocks = (t * TOP_K) // mblk1 + N_EXP
    dest, src, blk_idx, blk_exp, blk_nch, chunk_valid = _dispatch_plan(e_slot, N_EXP, mblk1, ch1, n_blocks)
    xs = gather_rows(h_t, src, chunk_valid, ch1)
    eo = grouped_swiglu(xs, moe_w_gate[0], moe_w_up[0], moe_w_down[0], blk_idx, blk_exp, blk_nch, mblk1, ch1,
                        FFN_TF, FFN_TK, FFN_TN, True)
    out_p, out_s = combine_norm(x, route, dest, eo, norm_final, tp)
    return (out_p.reshape(nb_p, len_p, d), out_s.reshape(nb_s, len_s, d), new_conv_prompt, new_conv_sample,
            new_chunk_v_sample)
```

```python
import functools

import jax
import jax.numpy as jnp
from jax import lax
from jax.experimental import pallas as pl
from jax.experimental.pallas import tpu as pltpu

F32 = jnp.float32
BF16 = jnp.bfloat16

D_MODEL = 2048
CONV_W = 31
CHUNK = 128
N_SG = 8
SG_DIM = D_MODEL // N_SG
N_EXP = 8
TOP_K = 2
EPS_RMS = 1e-6
EPS_LN = 1e-5

V7X_VMEM_BYTES = 64 * 1024 * 1024
VMEM_LIMIT = V7X_VMEM_BYTES - 8 * 1024 * 1024
LANES = 128
LANE_TILES = D_MODEL // LANES
HALO = 32
FFN_TF, FFN_TK, FFN_TN = 256, 1024, 1024
ROW_DMA_UNROLL = 8


def _params(*sem):
    return pltpu.CompilerParams(dimension_semantics=sem, vmem_limit_bytes=VMEM_LIMIT)


def _rms(x, g):
    return x * lax.rsqrt(jnp.mean(x * x, axis=-1, keepdims=True) + EPS_RMS) * g


def _layernorm(x, g, b):
    mu = jnp.mean(x, axis=-1, keepdims=True)
    xc = x - mu
    return xc * lax.rsqrt(jnp.mean(xc * xc, axis=-1, keepdims=True) + EPS_LN) * g + b


def _norm_kernel(x_ref, g_ref, h_ref):
    h_ref[...] = _rms(x_ref[...], g_ref[...]).astype(BF16)


def _add_norm_kernel(x_ref, d_ref, g_ref, xo_ref, h_ref):
    x = x_ref[...] + d_ref[...]
    xo_ref[...] = x
    h_ref[...] = _rms(x, g_ref[...]).astype(BF16)


def rmsnorm_bf16(x, g, delta=None, tr=512):
    t, d = x.shape
    row = pl.BlockSpec((tr, d), lambda i: (i, 0))
    vec = pl.BlockSpec((1, d), lambda i: (0, 0))
    g = g.reshape(1, d)
    if delta is None:
        return pl.pallas_call(
            _norm_kernel, grid=(t // tr,), in_specs=[row, vec], out_specs=row,
            out_shape=jax.ShapeDtypeStruct((t, d), BF16), compiler_params=_params("parallel"),
            name="rmsnorm")(x, g)
    return pl.pallas_call(
        _add_norm_kernel, grid=(t // tr,), in_specs=[row, row, vec], out_specs=[row, row],
        out_shape=[jax.ShapeDtypeStruct((t, d), F32), jax.ShapeDtypeStruct((t, d), BF16)],
        compiler_params=_params("parallel"), name="add_rmsnorm")(x, delta, g)


def _gelu_exact(z):
    return 0.5 * z * (1.0 + lax.erf(z * (0.5 ** 0.5)))


def _cast_on_first_row_tile(w_ref, ws_ref):
    @pl.when(pl.program_id(1) == 0)
    def _():
        ws_ref[...] = w_ref[...].astype(BF16)


def _mm_glu_kernel(a_ref, wa_ref, wg_ref, ba_ref, bg_ref, o_ref, was_ref, wgs_ref):
    _cast_on_first_row_tile(wa_ref, was_ref)
    _cast_on_first_row_tile(wg_ref, wgs_ref)
    act = a_ref[...]
    a = jnp.dot(act, was_ref[...], preferred_element_type=F32) + ba_ref[...]
    gate = jnp.dot(act, wgs_ref[...], preferred_element_type=F32) + bg_ref[...]
    glu = a * jax.nn.sigmoid(gate)
    for j in range(o_ref.shape[0]):
        o_ref[j] = glu[:, j * LANES:(j + 1) * LANES]


def _mm_gelu_kernel(a_ref, w_ref, b_ref, o_ref, ws_ref):
    _cast_on_first_row_tile(w_ref, ws_ref)
    z = jnp.dot(a_ref[...], ws_ref[...], preferred_element_type=F32) + b_ref[...]
    o_ref[...] = _gelu_exact(z)


def _mm_res_kernel(a_ref, w_ref, b_ref, r_ref, o_ref, ws_ref):
    _cast_on_first_row_tile(w_ref, ws_ref)
    z = jnp.dot(a_ref[...], ws_ref[...], preferred_element_type=F32) + b_ref[...]
    o_ref[...] = r_ref[...] + z


def matmul_glu(act, w, b, tm=1024, tn=512):
    t, k = act.shape
    n = w.shape[1] // 2
    nb = n // tn
    b = b.reshape(1, 2 * n)
    return pl.pallas_call(
        _mm_glu_kernel, grid=(nb, t // tm),
        in_specs=[pl.BlockSpec((tm, k), lambda j, i: (i, 0)),
                  pl.BlockSpec((k, tn), lambda j, i: (0, j)),
                  pl.BlockSpec((k, tn), lambda j, i: (0, j + nb)),
                  pl.BlockSpec((1, tn), lambda j, i: (0, j)),
                  pl.BlockSpec((1, tn), lambda j, i: (0, j + nb))],
        out_specs=pl.BlockSpec((tn // LANES, tm, LANES), lambda j, i: (j, i, 0)),
        out_shape=jax.ShapeDtypeStruct((n // LANES, t, LANES), F32),
        scratch_shapes=[pltpu.VMEM((k, tn), BF16), pltpu.VMEM((k, tn), BF16)],
        compiler_params=_params("parallel", "arbitrary"), name="matmul_glu")(act, w, w, b, b)


def matmul_gelu(act, w, b, tm=1024, tn=1024):
    t, k = act.shape
    n = w.shape[1]
    return pl.pallas_call(
        _mm_gelu_kernel, grid=(n // tn, t // tm),
        in_specs=[pl.BlockSpec((tm, k), lambda j, i: (i, 0)),
                  pl.BlockSpec((k, tn), lambda j, i: (0, j)),
                  pl.BlockSpec((1, tn), lambda j, i: (0, j))],
        out_specs=pl.BlockSpec((tm, tn), lambda j, i: (i, j)),
        out_shape=jax.ShapeDtypeStruct((t, n), F32),
        scratch_shapes=[pltpu.VMEM((k, tn), BF16)],
        compiler_params=_params("parallel", "arbitrary"), name="matmul_gelu")(act, w, b.reshape(1, n))


def matmul_residual(act, w, b, res, tm=1024, tn=1024):
    t, k = act.shape
    n = w.shape[1]
    return pl.pallas_call(
        _mm_res_kernel, grid=(n // tn, t // tm),
        in_specs=[pl.BlockSpec((tm, k), lambda j, i: (i, 0)),
                  pl.BlockSpec((k, tn), lambda j, i: (0, j)),
                  pl.BlockSpec((1, tn), lambda j, i: (0, j)),
                  pl.BlockSpec((tm, tn), lambda j, i: (i, j))],
        out_specs=pl.BlockSpec((tm, tn), lambda j, i: (i, j)),
        out_shape=jax.ShapeDtypeStruct((t, n), F32),
        scratch_shapes=[pltpu.VMEM((k, tn), BF16)],
        compiler_params=_params("parallel", "arbitrary"), name="matmul_residual")(
            act, w, b.reshape(1, n), res)


def _conv_taps(xp_ref, y_ref, wdw_ref, bdw_ref, l, x0, y0, tt, rows):
    first = HALO - (CONV_W - 1)
    for r0 in range(0, tt, rows):
        acc = jnp.broadcast_to(bdw_ref[l], (rows, LANES))
        for k in range(CONV_W):
            lo = x0 + r0 + first + k
            acc = acc + xp_ref[l, lo:lo + rows, :] * wdw_ref[l, k:k + 1, :]
        y_ref[l, y0 + r0:y0 + r0 + rows, :] = acc


def _ln_silu_store(y_ref, lng_ref, lnb_ref, o_ref):
    n = y_ref.shape[1]
    d = LANE_TILES * LANES
    s1 = y_ref[0]
    for l in range(1, LANE_TILES):
        s1 = s1 + y_ref[l]
    mu = jnp.sum(s1, axis=-1, keepdims=True) * (1.0 / d)
    s2 = jnp.zeros((n, LANES), F32)
    for l in range(LANE_TILES):
        c = y_ref[l] - mu
        s2 = s2 + c * c
    inv = lax.rsqrt(jnp.sum(s2, axis=-1, keepdims=True) * (1.0 / d) + EPS_LN)
    for l in range(LANE_TILES):
        y = (y_ref[l] - mu) * inv * lng_ref[l] + lnb_ref[l]
        o_ref[:, l * LANES:(l + 1) * LANES] = (y * jax.nn.sigmoid(y)).astype(BF16)


def _conv_prompt_kernel(halo_ref, main_ref, wdw_ref, bdw_ref, lng_ref, lnb_ref, o_ref, xp_ref, y_ref, *, tt):
    xp_ref[:, 0:HALO, :] = jnp.where(pl.program_id(1) > 0, halo_ref[...], 0.0)
    xp_ref[:, HALO:HALO + tt, :] = main_ref[...]

    def lane_tile(l, carry):
        _conv_taps(xp_ref, y_ref, wdw_ref, bdw_ref, l, 0, 0, tt, 32)
        return carry

    lax.fori_loop(0, LANE_TILES, lane_tile, 0)
    _ln_silu_store(y_ref, lng_ref, lnb_ref, o_ref)


def _conv_sample_kernel(hist_ref, main_ref, wdw_ref, bdw_ref, lng_ref, lnb_ref, o_ref, xp_ref, y_ref, *, ts, sb):
    span = HALO + ts
    for s in range(sb):
        xp_ref[:, s * span:s * span + HALO, :] = hist_ref[s]
        xp_ref[:, s * span + HALO:(s + 1) * span, :] = main_ref[:, s * ts:(s + 1) * ts, :]

    def lane_tile(l, carry):
        for s in range(sb):
            _conv_taps(xp_ref, y_ref, wdw_ref, bdw_ref, l, s * span, s * ts, ts, ts)
        return carry

    lax.fori_loop(0, LANE_TILES, lane_tile, 0)
    _ln_silu_store(y_ref, lng_ref, lnb_ref, o_ref)


def _lane_tiled(v):
    return v.reshape(v.shape[0], LANE_TILES, LANES).transpose(1, 0, 2)


def conv_ln_silu(glu_t, hist_t, w_dw, b_dw, ln_g, ln_b, n_prompt_seq, prompt_len, n_sample_seq, sample_len, tt=256):
    lt, t, _ = glu_t.shape
    d = lt * LANES
    tp = n_prompt_seq * prompt_len
    z3 = lambda *_: (0, 0, 0)
    wspecs = [pl.BlockSpec((lt, CONV_W, LANES), z3), pl.BlockSpec((lt, 1, LANES), z3),
              pl.BlockSpec((lt, 1, LANES), z3), pl.BlockSpec((lt, 1, LANES), z3)]
    wargs = (_lane_tiled(w_dw), _lane_tiled(b_dw.reshape(1, d)), _lane_tiled(ln_g.reshape(1, d)),
             _lane_tiled(ln_b.reshape(1, d)))
    tiles = prompt_len // tt
    per = tt // HALO
    y_p = pl.pallas_call(
        functools.partial(_conv_prompt_kernel, tt=tt), grid=(n_prompt_seq, tiles),
        in_specs=[pl.BlockSpec((lt, HALO, LANES), lambda b, i: (0, jnp.maximum((b * tiles + i) * per - 1, 0), 0)),
                  pl.BlockSpec((lt, tt, LANES), lambda b, i: (0, b * tiles + i, 0))] + wspecs,
        out_specs=pl.BlockSpec((tt, d), lambda b, i: (b * tiles + i, 0)),
        out_shape=jax.ShapeDtypeStruct((tp, d), BF16),
        scratch_shapes=[pltpu.VMEM((lt, HALO + tt, LANES), F32), pltpu.VMEM((lt, tt, LANES), F32)],
        compiler_params=_params("parallel", "parallel"), name="conv_prompt")(glu_t, glu_t, *wargs)
    ts = sample_len
    sb = 16
    base = tp // (sb * ts)
    y_s = pl.pallas_call(
        functools.partial(_conv_sample_kernel, ts=ts, sb=sb), grid=(n_sample_seq // sb,),
        in_specs=[pl.BlockSpec((sb, lt, HALO, LANES), lambda b: (b, 0, 0, 0)),
                  pl.BlockSpec((lt, sb * ts, LANES), lambda b: (0, base + b, 0))] + wspecs,
        out_specs=pl.BlockSpec((sb * ts, d), lambda b: (b, 0)),
        out_shape=jax.ShapeDtypeStruct((n_sample_seq * ts, d), BF16),
        scratch_shapes=[pltpu.VMEM((lt, sb * (HALO + ts), LANES), F32), pltpu.VMEM((lt, sb * ts, LANES), F32)],
        compiler_params=_params("parallel"), name="conv_sample")(hist_t, glu_t, *wargs)
    return jnp.concatenate([y_p, y_s], axis=0)


def _gmlp_mix_kernel(u_ref, v_ref, w_ref, bias_ref, lng_ref, lnb_ref, y_ref, vn_ref, *, n_prompt_chunks):
    vn = _layernorm(v_ref[...], lng_ref[...], lnb_ref[...])

    @pl.when(pl.program_id(0) >= n_prompt_chunks)
    def _():
        vn_ref[...] = vn

    vb = vn.astype(BF16)
    for g in range(N_SG):
        cols = slice(g * SG_DIM, (g + 1) * SG_DIM)
        mixed = jnp.dot(w_ref[0, g], vb[:, cols], preferred_element_type=F32) + bias_ref[0, :, cols]
        y_ref[:, cols] = (u_ref[:, cols] * mixed).astype(BF16)


def gmlp_mix(z, w_mix, bias_mix, ln_g, ln_b, n_prompt_chunks):
    t = z.shape[0]
    d = z.shape[1] // 2
    which = lambda c: (c >= n_prompt_chunks).astype(jnp.int32)
    return pl.pallas_call(
        functools.partial(_gmlp_mix_kernel, n_prompt_chunks=n_prompt_chunks), grid=(t // CHUNK,),
        in_specs=[pl.BlockSpec((CHUNK, d), lambda c: (c, 0)),
                  pl.BlockSpec((CHUNK, d), lambda c: (c, 1)),
                  pl.BlockSpec((1, N_SG, CHUNK, CHUNK), lambda c: (which(c), 0, 0, 0)),
                  pl.BlockSpec((1, CHUNK, d), lambda c: (which(c), 0, 0)),
                  pl.BlockSpec((1, d), lambda c: (0, 0)),
                  pl.BlockSpec((1, d), lambda c: (0, 0))],
        out_specs=[pl.BlockSpec((CHUNK, d), lambda c: (c, 0)),
                   pl.BlockSpec((CHUNK, d), lambda c: (jnp.maximum(c - n_prompt_chunks, 0), 0))],
        out_shape=[jax.ShapeDtypeStruct((t, d), BF16), jax.ShapeDtypeStruct((t - n_prompt_chunks * CHUNK, d), F32)],
        compiler_params=_params("arbitrary"), name="gmlp_mix")(
            z, z, w_mix, bias_mix, ln_g.reshape(1, d), ln_b.reshape(1, d))


def _ffn_kernel(blk_ref, exp_ref, nch_ref, x_ref, wg_ref, wu_ref, wd_ref, o_ref, a_ref, acc_ref, *,
                n_chunks, ch, nf, nk, tiled):
    b = pl.program_id(0)
    s = pl.program_id(1)
    nch = nch_ref[b]
    tf = wg_ref.shape[-1]
    tk = wd_ref.shape[1]
    kk = lax.rem(jnp.maximum(s - nf, 0), nk)

    def store_out(rows, val):
        o_ref[rows] = val.reshape((rows.size,) + o_ref.shape[1:]) if tiled else val

    def up_gate(rows):
        xr = x_ref[rows, :]
        g = jnp.dot(xr, wg_ref[0].astype(BF16), preferred_element_type=F32)
        u = jnp.dot(xr, wu_ref[0].astype(BF16), preferred_element_type=F32)
        a_ref[rows, pl.ds(pl.multiple_of(s * tf, tf), tf)] = (g * jax.nn.sigmoid(g) * u).astype(BF16)

    def down(rows):
        slab = pl.ds(pl.multiple_of(kk * tk, tk), tk)
        part = jnp.dot(a_ref[rows, slab], wd_ref[0].astype(BF16), preferred_element_type=F32)
        acc_ref[rows, :] = part + jnp.where(kk > 0, acc_ref[rows, :], 0.0)

        @pl.when(kk == nk - 1)
        def _():
            store_out(rows, acc_ref[rows, :])

    @pl.when(jnp.logical_and(b == 0, s == 0))
    def _():
        acc_ref[...] = jnp.zeros(acc_ref.shape, F32)

    def run(step_fn):
        @pl.when(nch == n_chunks)
        def _():
            step_fn(pl.ds(0, n_chunks * ch))

        @pl.when(jnp.logical_and(nch > 0, nch < n_chunks))
        def _():
            for c in range(n_chunks - 1):
                @pl.when(c < nch)
                def _():
                    step_fn(pl.ds(c * ch, ch))

    @pl.when(s < nf)
    def _():
        run(up_gate)

    @pl.when(s >= nf)
    def _():
        run(down)

    @pl.when(jnp.logical_and(s >= nf, kk == nk - 1))
    def _():
        for c in range(n_chunks):
            @pl.when(c >= nch)
            def _():
                store_out(pl.ds(c * ch, ch), jnp.zeros((ch, acc_ref.shape[-1]), F32))


def grouped_swiglu(x, wg, wu, wd, blk_idx, blk_exp, blk_nch, mblk, ch, tf, tk, tn, tiled):
    p, d = x.shape
    dff = wg.shape[-1]
    nb = blk_idx.shape[0]
    nf, nk, nn = dff // tf, dff // tk, d // tn
    up_tile = lambda b, s, nch: jnp.where(nch[b] > 0, jnp.minimum(s, nf - 1), nf - 1)
    down_step = lambda b, s, nch: jnp.where(nch[b] > 0, jnp.maximum(s - nf, 0), nn * nk - 1)
    out_tile = lambda s: jnp.maximum(s - nf, 0) // nk
    once = dict(pipeline_mode=pl.Buffered(1))
    x_spec = pl.BlockSpec((mblk, d), lambda b, s, blk, ex, nch: (blk[b], 0), **once)
    if tiled:
        out_spec = pl.BlockSpec((mblk, tn // LANES, LANES), lambda b, s, blk, ex, nch: (b, out_tile(s), 0), **once)
        out_shape = jax.ShapeDtypeStruct((p, d // LANES, LANES), F32)
    else:
        out_spec = pl.BlockSpec((mblk, tn), lambda b, s, blk, ex, nch: (b, out_tile(s)), **once)
        out_shape = jax.ShapeDtypeStruct((p, d), F32)
    grid_spec = pltpu.PrefetchScalarGridSpec(
        num_scalar_prefetch=3, grid=(nb, nf + nn * nk),
        in_specs=[x_spec,
                  pl.BlockSpec((1, d, tf), lambda b, s, blk, ex, nch: (ex[b], 0, up_tile(b, s, nch))),
                  pl.BlockSpec((1, d, tf), lambda b, s, blk, ex, nch: (ex[b], 0, up_tile(b, s, nch))),
                  pl.BlockSpec((1, tk, tn), lambda b, s, blk, ex, nch: (
                      ex[b], lax.rem(down_step(b, s, nch), nk), down_step(b, s, nch) // nk))],
        out_specs=out_spec,
        scratch_shapes=[pltpu.VMEM((mblk, dff), BF16), pltpu.VMEM((mblk, tn), F32)])
    return pl.pallas_call(
        functools.partial(_ffn_kernel, n_chunks=mblk // ch, ch=ch, nf=nf, nk=nk, tiled=tiled), grid_spec=grid_spec,
        out_shape=out_shape, compiler_params=_params("arbitrary", "arbitrary"),
        name="grouped_swiglu_tiled" if tiled else "grouped_swiglu")(blk_idx, blk_exp, blk_nch, x, wg, wu, wd)


def _norm_route_kernel(x_ref, g_ref, r_ref, h_ref, route_ref):
    h = _rms(x_ref[...], g_ref[...])
    h_ref[...] = h.astype(BF16).reshape(h_ref.shape)
    logits = jnp.dot(h, r_ref[...], preferred_element_type=F32, precision=lax.Precision.HIGHEST)
    lane = lax.broadcasted_iota(jnp.int32, logits.shape, 1).astype(F32)
    neg = jnp.float32(-jnp.inf)
    logits = jnp.where(lane < N_EXP, logits, neg)
    m1 = jnp.max(logits, axis=-1, keepdims=True)
    i1 = jnp.min(jnp.where(logits == m1, lane, float(LANES)), axis=-1, keepdims=True)
    rest = jnp.where(lane == i1, neg, logits)
    m2 = jnp.max(rest, axis=-1, keepdims=True)
    i2 = jnp.min(jnp.where(rest == m2, lane, float(LANES)), axis=-1, keepdims=True)
    e2 = jnp.exp(m2 - m1)
    g1 = 1.0 / (1.0 + e2)
    g2 = e2 / (1.0 + e2)
    route = jnp.where(lane == 0, i1, 0.0)
    route = jnp.where(lane == 1, i2, route)
    route = jnp.where(lane == 2, g1, route)
    route = jnp.where(lane == 3, g2, route)
    route_ref[...] = route


def norm_route(x, g, router, tr=512):
    t, d = x.shape
    r = jnp.zeros((d, LANES), F32).at[:, :N_EXP].set(router)
    return pl.pallas_call(
        _norm_route_kernel, grid=(t // tr,),
        in_specs=[pl.BlockSpec((tr, d), lambda i: (i, 0)), pl.BlockSpec((1, d), lambda i: (0, 0)),
                  pl.BlockSpec((d, LANES), lambda i: (0, 0))],
        out_specs=[pl.BlockSpec((tr, d // LANES, LANES), lambda i: (i, 0, 0)),
                   pl.BlockSpec((tr, LANES), lambda i: (i, 0))],
        out_shape=[jax.ShapeDtypeStruct((t, d // LANES, LANES), BF16), jax.ShapeDtypeStruct((t, LANES), F32)],
        compiler_params=_params("parallel"), name="norm_route")(x, g.reshape(1, d), r)


def _gather_rows_kernel(src_ref, valid_ref, h_ref, o_ref, buf_ref, sem, *, rows):
    c = pl.program_id(0)
    last = pl.num_programs(0) - 1
    slot = lax.rem(c, 2)

    def request(j, slot_j):
        def body(i, carry):
            for u in range(ROW_DMA_UNROLL):
                r = i * ROW_DMA_UNROLL + u
                pltpu.make_async_copy(h_ref.at[pl.ds(src_ref[j * rows + r], 1)],
                                      buf_ref.at[slot_j, pl.ds(r, 1)], sem.at[slot_j]).start(priority=u % 2)
            return carry

        lax.fori_loop(0, rows // ROW_DMA_UNROLL, body, 0)

    @pl.when(jnp.logical_and(c == 0, valid_ref[0] > 0))
    def _():
        request(0, 0)

    nxt = jnp.minimum(c + 1, last)

    @pl.when(jnp.logical_and(c < last, valid_ref[nxt] > 0))
    def _():
        request(nxt, 1 - slot)

    @pl.when(valid_ref[c] > 0)
    def _():
        pltpu.make_async_copy(h_ref.at[pl.ds(0, rows)], buf_ref.at[slot], sem.at[slot]).wait()
        o_ref[...] = buf_ref[slot].reshape(o_ref.shape)

    @pl.when(valid_ref[c] == 0)
    def _():
        o_ref[...] = jnp.zeros(o_ref.shape, o_ref.dtype)


def gather_rows(h, src, valid, rows):
    p = src.shape[0]
    d = h.shape[1] * h.shape[2]
    grid_spec = pltpu.PrefetchScalarGridSpec(
        num_scalar_prefetch=2, grid=(p // rows,),
        in_specs=[pl.BlockSpec(memory_space=pl.ANY)],
        out_specs=pl.BlockSpec((rows, d), lambda c, src, valid: (c, 0)),
        scratch_shapes=[pltpu.VMEM((2, rows) + h.shape[1:], h.dtype), pltpu.SemaphoreType.DMA((2,))])
    return pl.pallas_call(
        functools.partial(_gather_rows_kernel, rows=rows), grid_spec=grid_spec,
        out_shape=jax.ShapeDtypeStruct((p, d), h.dtype),
        compiler_params=_params("arbitrary"), name="gather_rows")(src, valid, h)


def _combine_norm_kernel(pos_ref, x_ref, route_ref, g_ref, o_hbm, outp_ref, outs_ref, r_ref, sem, *, tc, n_first):
    i = pl.program_id(0)
    last = pl.num_programs(0) - 1
    slot = lax.rem(i, 2)

    def request(j, slot_j):
        def body(q, carry):
            for u in range(ROW_DMA_UNROLL // TOP_K):
                r = q * (ROW_DMA_UNROLL // TOP_K) + u
                for k in range(TOP_K):
                    pltpu.make_async_copy(o_hbm.at[pl.ds(pos_ref[TOP_K * (j * tc + r) + k], 1)],
                                          r_ref.at[slot_j, k, pl.ds(r, 1)], sem.at[slot_j]).start(priority=k % 2)
            return carry

        lax.fori_loop(0, tc * TOP_K // ROW_DMA_UNROLL, body, 0)

    @pl.when(i == 0)
    def _():
        request(0, 0)

    @pl.when(i < last)
    def _():
        request(i + 1, 1 - slot)

    for k in range(TOP_K):
        pltpu.make_async_copy(o_hbm.at[pl.ds(0, tc)], r_ref.at[slot, k], sem.at[slot]).wait()
    route = route_ref[...]
    g1 = route[:, 2:3]
    g2 = route[:, 3:4]
    d = x_ref.shape[-1]
    y = x_ref[...] + (g1 * r_ref[slot, 0].reshape(tc, d) + g2 * r_ref[slot, 1].reshape(tc, d))
    res = _rms(y, g_ref[...])

    @pl.when(i < n_first)
    def _():
        outp_ref[...] = res

    @pl.when(i >= n_first)
    def _():
        outs_ref[...] = res


def combine_norm(x, route, pos, expert_out, g, t_first, tc=256):
    t, d = x.shape
    tile = expert_out.shape[1:]
    n_first = t_first // tc
    grid_spec = pltpu.PrefetchScalarGridSpec(
        num_scalar_prefetch=1, grid=(t // tc,),
        in_specs=[pl.BlockSpec((tc, d), lambda i, pos: (i, 0)),
                  pl.BlockSpec((tc, LANES), lambda i, pos: (i, 0)),
                  pl.BlockSpec((1, d), lambda i, pos: (0, 0)),
                  pl.BlockSpec(memory_space=pl.ANY)],
        out_specs=[pl.BlockSpec((tc, d), lambda i, pos: (jnp.minimum(i, n_first - 1), 0)),
                   pl.BlockSpec((tc, d), lambda i, pos: (jnp.maximum(i - n_first, 0), 0))],
        scratch_shapes=[pltpu.VMEM((2, TOP_K, tc) + tile, F32), pltpu.SemaphoreType.DMA((2,))])
    return pl.pallas_call(
        functools.partial(_combine_norm_kernel, tc=tc, n_first=n_first), grid_spec=grid_spec,
        out_shape=[jax.ShapeDtypeStruct((t_first, d), F32), jax.ShapeDtypeStruct((t - t_first, d), F32)],
        compiler_params=_params("arbitrary"), name="combine_norm")(pos, x, route, g.reshape(1, d), expert_out)


def _dispatch_plan(e_slot, n_exp, mblk, ch, n_blocks):
    n_slots = e_slot.shape[0]
    onehot = (e_slot[:, None] == jnp.arange(n_exp, dtype=jnp.int32)[None, :]).astype(jnp.int32)
    seg = 512
    oh = onehot.astype(F32).reshape(n_slots // seg, seg, n_exp)
    within = jnp.einsum("ts,bse->bte", jnp.tril(jnp.ones((seg, seg), F32)), oh, precision=lax.Precision.HIGHEST)
    totals = within[:, -1, :]
    csum = (within + (jnp.cumsum(totals, axis=0) - totals)[:, None, :]).reshape(n_slots, n_exp).astype(jnp.int32)
    rank = jnp.sum(csum * onehot, axis=1) - 1
    counts = csum[-1]
    nblk = (counts + mblk - 1) // mblk
    blk_end = jnp.cumsum(nblk)
    blk_start = blk_end - nblk
    dest = blk_start[e_slot] * mblk + rank
    n_used = blk_end[-1]
    step = jnp.arange(n_blocks, dtype=jnp.int32)
    blk_idx = jnp.minimum(step, n_used - 1)
    blk_exp = jnp.minimum(jnp.sum((blk_idx[:, None] >= blk_end[None, :]).astype(jnp.int32), axis=1), n_exp - 1)
    rows_in = jnp.clip(counts[blk_exp] - (blk_idx - blk_start[blk_exp]) * mblk, 0, mblk)
    blk_nch = jnp.where(step < n_used, (rows_in + ch - 1) // ch, 0).astype(jnp.int32)
    src = jnp.zeros((n_blocks * mblk,), jnp.int32).at[dest].set(jnp.arange(n_slots, dtype=jnp.int32) // TOP_K)
    cpb = mblk // ch
    chunk_valid = (jnp.arange(cpb, dtype=jnp.int32)[None, :] < blk_nch[:, None]).astype(jnp.int32).reshape(-1)
    return dest.astype(jnp.int32), src, blk_idx.astype(jnp.int32), blk_exp.astype(jnp.int32), blk_nch, chunk_valid


def kernel(x_prompt, x_sample, state_conv, norm_mix, norm_ffn, norm_final, conv_w_in, conv_b_in, conv_w_dw, conv_b_dw, conv_ln_g, conv_ln_b, conv_w_out, conv_b_out, gmlp_w_in, gmlp_b_in, gmlp_ln_g, gmlp_ln_b, gmlp_w_s, gmlp_b_s, gmlp_w_out, gmlp_b_out, ffn_w_gate, ffn_w_up, ffn_w_down, moe_router, moe_w_gate, moe_w_up, moe_w_down):
    nb_p, len_p, d = x_prompt.shape
    nb_s, len_s, _ = x_sample.shape
    tp, ts = nb_p * len_p, nb_s * len_s
    t = tp + ts
    hist_rows = CONV_W - 1
    x = jnp.concatenate([x_prompt.reshape(tp, d), x_sample.reshape(ts, d)], axis=0)

    h = rmsnorm_bf16(x, norm_mix[0])
    glu_t = matmul_glu(h, conv_w_in[0], conv_b_in[0])
    hist = jnp.pad(state_conv[0], ((0, 0), (HALO - hist_rows, 0), (0, 0)))
    hist_t = hist.reshape(nb_s, HALO, LANE_TILES, LANES).transpose(0, 2, 1, 3)
    y = conv_ln_silu(glu_t, hist_t, conv_w_dw[0], conv_b_dw[0], conv_ln_g[0], conv_ln_b[0], nb_p, len_p, nb_s, len_s)
    x = matmul_residual(y, conv_w_out[0], conv_b_out[0], x)
    tails = jnp.stack([glu_t[:, (b + 1) * len_p - hist_rows:(b + 1) * len_p] for b in range(nb_p)])
    new_conv_prompt = tails.transpose(0, 2, 1, 3).reshape(nb_p, hist_rows, d)[None]
    glu_s = glu_t[:, tp:].reshape(LANE_TILES, nb_s, len_s, LANES).transpose(1, 2, 0, 3).reshape(nb_s, len_s, d)
    new_conv_sample = jnp.concatenate([state_conv[0], glu_s], axis=1)[:, len_s:][None]

    h = rmsnorm_bf16(x, norm_ffn[0])
    mblk0, ch0 = 1152, 576
    nb0 = t // mblk0
    ffn = grouped_swiglu(h, ffn_w_gate, ffn_w_up, ffn_w_down,
                         jnp.arange(nb0, dtype=jnp.int32), jnp.zeros((nb0,), jnp.int32),
                         jnp.full((nb0,), mblk0 // ch0, jnp.int32), mblk0, ch0, FFN_TF, FFN_TK, FFN_TN, False)

    x, h = rmsnorm_bf16(x, norm_mix[1], delta=ffn)
    z = matmul_gelu(h, gmlp_w_in[0], gmlp_b_in[0])
    tril = jnp.tril(jnp.ones((CHUNK, CHUNK), dtype=bool))
    w_prompt = jnp.where(tril[None], gmlp_w_s[0], 0)
    seqs = CHUNK // len_s
    w_small = jnp.where(tril[None, :len_s, :len_s], gmlp_w_s[0][:, :len_s, :len_s], 0)
    w_sample = jnp.einsum("ab,gts->gatbs", jnp.eye(seqs, dtype=F32), w_small).reshape(N_SG, CHUNK, CHUNK)
    w_mix = jnp.stack([w_prompt, w_sample]).astype(BF16)
    bias_rows = jnp.stack([gmlp_b_s[0].T, jnp.tile(gmlp_b_s[0].T[:len_s], (seqs, 1))])
    bias_mix = jnp.repeat(bias_rows, SG_DIM, axis=2)
    y, v = gmlp_mix(z, w_mix, bias_mix, gmlp_ln_g[0], gmlp_ln_b[0], tp // CHUNK)
    x = matmul_residual(y, gmlp_w_out[0], gmlp_b_out[0], x)
    new_chunk_v_sample = v.reshape(nb_s, len_s, d)[None]

    h_t, route = norm_route(x, norm_ffn[1], moe_router[0])
    e_slot = route[:, :TOP_K].astype(jnp.int32).reshape(-1)
    mblk1, ch1 = 1280, 640
    n_blocks = (t * TOP_K) // mblk1 + N_EXP
    dest, src, blk_idx, blk_exp, blk_nch, chunk_valid = _dispatch_plan(e_slot, N_EXP, mblk1, ch1, n_blocks)
    xs = gather_rows(h_t, src, chunk_valid, ch1)
    eo = grouped_swiglu(xs, moe_w_gate[0], moe_w_up[0], moe_w_down[0], blk_idx, blk_exp, blk_nch, mblk1, ch1,
                        FFN_TF, FFN_TK, FFN_TN, True)
    out_p, out_s = combine_norm(x, route, dest, eo, norm_final, tp)
    return (out_p.reshape(nb_p, len_p, d), out_s.reshape(nb_s, len_s, d), new_conv_prompt, new_conv_sample,
            new_chunk_v_sample)
```

```python
import functools

import jax
import jax.numpy as jnp
from jax import lax
from jax.experimental import pallas as pl
from jax.experimental.pallas import tpu as pltpu

F32 = jnp.float32
BF16 = jnp.bfloat16

D_MODEL = 2048
CONV_W = 31
CHUNK = 128
N_SG = 8
SG_DIM = D_MODEL // N_SG
N_EXP = 8
TOP_K = 2
EPS_RMS = 1e-6
EPS_LN = 1e-5

V7X_VMEM_BYTES = 64 * 1024 * 1024
VMEM_LIMIT = V7X_VMEM_BYTES - 8 * 1024 * 1024
LANES = 128
LANE_TILES = D_MODEL // LANES
HALO = 32
FFN_TF, FFN_TK, FFN_TN = 256, 1024, 1024
ROW_DMA_UNROLL = 8


def _params(*sem):
    return pltpu.CompilerParams(dimension_semantics=sem, vmem_limit_bytes=VMEM_LIMIT)


def _rms(x, g):
    return x * lax.rsqrt(jnp.mean(x * x, axis=-1, keepdims=True) + EPS_RMS) * g


def _layernorm(x, g, b):
    mu = jnp.mean(x, axis=-1, keepdims=True)
    xc = x - mu
    return xc * lax.rsqrt(jnp.mean(xc * xc, axis=-1, keepdims=True) + EPS_LN) * g + b


def _concat_norm_kernel(xa_ref, xb_ref, g_ref, xo_ref, h_ref, *, n_first):
    def emit(x):
        xo_ref[...] = x
        h_ref[...] = _rms(x, g_ref[...]).astype(BF16)

    @pl.when(pl.program_id(0) < n_first)
    def _():
        emit(xa_ref[...])

    @pl.when(pl.program_id(0) >= n_first)
    def _():
        emit(xb_ref[...])


def _add_norm_kernel(x_ref, d_ref, g_ref, xo_ref, h_ref):
    x = x_ref[...] + d_ref[...]
    xo_ref[...] = x
    h_ref[...] = _rms(x, g_ref[...]).astype(BF16)


def concat_rmsnorm(xa, xb, g, tr=512):
    ta, d = xa.shape
    t = ta + xb.shape[0]
    n_first = ta // tr
    row = pl.BlockSpec((tr, d), lambda i: (i, 0))
    return pl.pallas_call(
        functools.partial(_concat_norm_kernel, n_first=n_first), grid=(t // tr,),
        in_specs=[pl.BlockSpec((tr, d), lambda i: (jnp.minimum(i, n_first - 1), 0)),
                  pl.BlockSpec((tr, d), lambda i: (jnp.maximum(i - n_first, 0), 0)),
                  pl.BlockSpec((1, d), lambda i: (0, 0))],
        out_specs=[row, row],
        out_shape=[jax.ShapeDtypeStruct((t, d), F32), jax.ShapeDtypeStruct((t, d), BF16)],
        compiler_params=_params("arbitrary"), name="concat_rmsnorm")(xa, xb, g.reshape(1, d))


def add_rmsnorm(x, delta, g, tr=512):
    t, d = x.shape
    row = pl.BlockSpec((tr, d), lambda i: (i, 0))
    return pl.pallas_call(
        _add_norm_kernel, grid=(t // tr,), in_specs=[row, row, pl.BlockSpec((1, d), lambda i: (0, 0))],
        out_specs=[row, row],
        out_shape=[jax.ShapeDtypeStruct((t, d), F32), jax.ShapeDtypeStruct((t, d), BF16)],
        compiler_params=_params("parallel"), name="add_rmsnorm")(x, delta, g.reshape(1, d))


def _gelu_exact(z):
    return 0.5 * z * (1.0 + lax.erf(z * (0.5 ** 0.5)))


def _cast_on_first_row_tile(w_ref, ws_ref):
    @pl.when(pl.program_id(1) == 0)
    def _():
        ws_ref[...] = w_ref[...].astype(BF16)


def _mm_glu_kernel(a_ref, wa_ref, wg_ref, ba_ref, bg_ref, o_ref, was_ref, wgs_ref):
    _cast_on_first_row_tile(wa_ref, was_ref)
    _cast_on_first_row_tile(wg_ref, wgs_ref)
    act = a_ref[...]
    a = jnp.dot(act, was_ref[...], preferred_element_type=F32) + ba_ref[...]
    gate = jnp.dot(act, wgs_ref[...], preferred_element_type=F32) + bg_ref[...]
    glu = a * jax.nn.sigmoid(gate)
    for j in range(o_ref.shape[0]):
        o_ref[j] = glu[:, j * LANES:(j + 1) * LANES]


def _mm_gelu_kernel(a_ref, w_ref, b_ref, o_ref, ws_ref):
    _cast_on_first_row_tile(w_ref, ws_ref)
    z = jnp.dot(a_ref[...], ws_ref[...], preferred_element_type=F32) + b_ref[...]
    o_ref[...] = _gelu_exact(z)


def _top2_route(h, r_ref):
    r = r_ref[...]
    h_hi = h.astype(BF16)
    h_lo = (h - h_hi.astype(F32)).astype(BF16)
    r_hi = r.astype(BF16)
    r_lo = (r - r_hi.astype(F32)).astype(BF16)
    logits = (jnp.dot(h_hi, r_hi, preferred_element_type=F32) + jnp.dot(h_hi, r_lo, preferred_element_type=F32)
              + jnp.dot(h_lo, r_hi, preferred_element_type=F32))
    lane = lax.broadcasted_iota(jnp.int32, logits.shape, 1).astype(F32)
    neg = jnp.float32(-jnp.inf)
    logits = jnp.where(lane < N_EXP, logits, neg)
    m1 = jnp.max(logits, axis=-1, keepdims=True)
    i1 = jnp.min(jnp.where(logits == m1, lane, float(LANES)), axis=-1, keepdims=True)
    rest = jnp.where(lane == i1, neg, logits)
    m2 = jnp.max(rest, axis=-1, keepdims=True)
    i2 = jnp.min(jnp.where(rest == m2, lane, float(LANES)), axis=-1, keepdims=True)
    e2 = jnp.exp(m2 - m1)
    g1 = 1.0 / (1.0 + e2)
    g2 = e2 / (1.0 + e2)
    route = jnp.where(lane == 0, i1, 0.0)
    route = jnp.where(lane == 1, i2, route)
    route = jnp.where(lane == 2, g1, route)
    return jnp.where(lane == 3, g2, route)


def _mm_res_kernel(a_ref, w_ref, b_ref, r_ref, *rest, norm):
    if norm:
        g_ref, x_ref, h_ref, ws_ref = rest
    else:
        x_ref, ws_ref = rest

    @pl.when(pl.program_id(0) == 0)
    def _():
        ws_ref[...] = w_ref[...].astype(BF16)

    x = r_ref[...] + (jnp.dot(a_ref[...], ws_ref[...], preferred_element_type=F32) + b_ref[...])
    x_ref[...] = x
    if norm:
        h_ref[...] = _rms(x, g_ref[...]).astype(BF16)


def _norm_route_kernel(x_ref, g_ref, rt_ref, h_ref, route_ref):
    h = _rms(x_ref[...], g_ref[...])
    h_ref[...] = h.astype(BF16).reshape(h_ref.shape)
    route_ref[...] = _top2_route(h, rt_ref)


def matmul_glu(act, w, b, tm=1024, tn=512):
    t, k = act.shape
    n = w.shape[1] // 2
    nb = n // tn
    b = b.reshape(1, 2 * n)
    return pl.pallas_call(
        _mm_glu_kernel, grid=(nb, t // tm),
        in_specs=[pl.BlockSpec((tm, k), lambda j, i: (i, 0)),
                  pl.BlockSpec((k, tn), lambda j, i: (0, j)),
                  pl.BlockSpec((k, tn), lambda j, i: (0, j + nb)),
                  pl.BlockSpec((1, tn), lambda j, i: (0, j)),
                  pl.BlockSpec((1, tn), lambda j, i: (0, j + nb))],
        out_specs=pl.BlockSpec((tn // LANES, tm, LANES), lambda j, i: (j, i, 0)),
        out_shape=jax.ShapeDtypeStruct((n // LANES, t, LANES), F32),
        scratch_shapes=[pltpu.VMEM((k, tn), BF16), pltpu.VMEM((k, tn), BF16)],
        compiler_params=_params("parallel", "arbitrary"), name="matmul_glu")(act, w, w, b, b)


def matmul_gelu(act, w, b, tm=1024, tn=1024):
    t, k = act.shape
    n = w.shape[1]
    return pl.pallas_call(
        _mm_gelu_kernel, grid=(n // tn, t // tm),
        in_specs=[pl.BlockSpec((tm, k), lambda j, i: (i, 0)),
                  pl.BlockSpec((k, tn), lambda j, i: (0, j)),
                  pl.BlockSpec((1, tn), lambda j, i: (0, j))],
        out_specs=pl.BlockSpec((tm, tn), lambda j, i: (i, j)),
        out_shape=jax.ShapeDtypeStruct((t, n), F32),
        scratch_shapes=[pltpu.VMEM((k, tn), BF16)],
        compiler_params=_params("parallel", "arbitrary"), name="matmul_gelu")(act, w, b.reshape(1, n))


def matmul_residual(act, w, b, res, g=None, tm=512):
    t, k = act.shape
    n = w.shape[1]
    norm = g is not None
    row = lambda width: pl.BlockSpec((tm, width), lambda i: (i, 0))
    fixed = lambda shape: pl.BlockSpec(shape, lambda i: (0, 0))
    in_specs = [row(k), pl.BlockSpec((k, n), lambda i: (0, 0), pipeline_mode=pl.Buffered(1)), fixed((1, n)), row(n)]
    args = [act, w, b.reshape(1, n), res]
    out_specs, out_shape = row(n), jax.ShapeDtypeStruct((t, n), F32)
    if norm:
        in_specs.append(fixed((1, n)))
        args.append(g.reshape(1, n))
        out_specs, out_shape = [out_specs, row(n)], [out_shape, jax.ShapeDtypeStruct((t, n), BF16)]
    return pl.pallas_call(
        functools.partial(_mm_res_kernel, norm=norm), grid=(t // tm,),
        in_specs=in_specs, out_specs=out_specs, out_shape=out_shape,
        scratch_shapes=[pltpu.VMEM((k, n), BF16)],
        compiler_params=_params("arbitrary"),
        name="matmul_residual_norm" if norm else "matmul_residual")(*args)


def norm_route(x, g, router, tr=512):
    t, d = x.shape
    r = jnp.zeros((d, LANES), F32).at[:, :N_EXP].set(router)
    return pl.pallas_call(
        _norm_route_kernel, grid=(t // tr,),
        in_specs=[pl.BlockSpec((tr, d), lambda i: (i, 0)), pl.BlockSpec((1, d), lambda i: (0, 0)),
                  pl.BlockSpec((d, LANES), lambda i: (0, 0))],
        out_specs=[pl.BlockSpec((tr, d // LANES, LANES), lambda i: (i, 0, 0)),
                   pl.BlockSpec((tr, LANES), lambda i: (i, 0))],
        out_shape=[jax.ShapeDtypeStruct((t, d // LANES, LANES), BF16), jax.ShapeDtypeStruct((t, LANES), F32)],
        compiler_params=_params("parallel"), name="norm_route")(x, g.reshape(1, d), r)


def _conv_taps(xp_ref, y_ref, wdw_ref, bdw_ref, l, x0, y0, tt, rows):
    first = HALO - (CONV_W - 1)
    for r0 in range(0, tt, rows):
        acc = jnp.broadcast_to(bdw_ref[l], (rows, LANES))
        for k in range(CONV_W):
            lo = x0 + r0 + first + k
            acc = acc + xp_ref[l, lo:lo + rows, :] * wdw_ref[l, k:k + 1, :]
        y_ref[l, y0 + r0:y0 + r0 + rows, :] = acc


def _ln_silu_store(y_ref, lng_ref, lnb_ref, o_ref):
    n = y_ref.shape[1]
    d = LANE_TILES * LANES
    s1 = y_ref[0]
    for l in range(1, LANE_TILES):
        s1 = s1 + y_ref[l]
    mu = jnp.sum(s1, axis=-1, keepdims=True) * (1.0 / d)
    s2 = jnp.zeros((n, LANES), F32)
    for l in range(LANE_TILES):
        c = y_ref[l] - mu
        s2 = s2 + c * c
    inv = lax.rsqrt(jnp.sum(s2, axis=-1, keepdims=True) * (1.0 / d) + EPS_LN)
    for l in range(LANE_TILES):
        y = (y_ref[l] - mu) * inv * lng_ref[l] + lnb_ref[l]
        o_ref[:, l * LANES:(l + 1) * LANES] = (y * jax.nn.sigmoid(y)).astype(BF16)


def _conv_prompt_kernel(halo_ref, main_ref, wdw_ref, bdw_ref, lng_ref, lnb_ref, o_ref, xp_ref, y_ref, *, tt):
    xp_ref[:, 0:HALO, :] = jnp.where(pl.program_id(1) > 0, halo_ref[...], 0.0)
    xp_ref[:, HALO:HALO + tt, :] = main_ref[...]

    def lane_tile(l, carry):
        _conv_taps(xp_ref, y_ref, wdw_ref, bdw_ref, l, 0, 0, tt, 32)
        return carry

    lax.fori_loop(0, LANE_TILES, lane_tile, 0)
    _ln_silu_store(y_ref, lng_ref, lnb_ref, o_ref)


def _conv_sample_kernel(hist_ref, main_ref, wdw_ref, bdw_ref, lng_ref, lnb_ref, o_ref, xp_ref, y_ref, *, ts, sb):
    span = HALO + ts
    pad = HALO - hist_ref.shape[1]
    for s in range(sb):
        xp_ref[:, s * span:s * span + pad, :] = jnp.zeros((LANE_TILES, pad, LANES), F32)
        for l in range(LANE_TILES):
            xp_ref[l, s * span + pad:s * span + HALO, :] = hist_ref[s, :, l * LANES:(l + 1) * LANES]
        xp_ref[:, s * span + HALO:(s + 1) * span, :] = main_ref[:, s * ts:(s + 1) * ts, :]

    def lane_tile(l, carry):
        for s in range(sb):
            _conv_taps(xp_ref, y_ref, wdw_ref, bdw_ref, l, s * span, s * ts, ts, ts)
        return carry

    lax.fori_loop(0, LANE_TILES, lane_tile, 0)
    _ln_silu_store(y_ref, lng_ref, lnb_ref, o_ref)


def _lane_tiled(v):
    return v.reshape(v.shape[0], LANE_TILES, LANES).transpose(1, 0, 2)


def conv_ln_silu(glu_t, hist, w_dw, b_dw, ln_g, ln_b, n_prompt_seq, prompt_len, n_sample_seq, sample_len, tt=256):
    lt, t, _ = glu_t.shape
    d = lt * LANES
    tp = n_prompt_seq * prompt_len
    z3 = lambda *_: (0, 0, 0)
    wspecs = [pl.BlockSpec((lt, CONV_W, LANES), z3), pl.BlockSpec((lt, 1, LANES), z3),
              pl.BlockSpec((lt, 1, LANES), z3), pl.BlockSpec((lt, 1, LANES), z3)]
    wargs = (_lane_tiled(w_dw), _lane_tiled(b_dw.reshape(1, d)), _lane_tiled(ln_g.reshape(1, d)),
             _lane_tiled(ln_b.reshape(1, d)))
    tiles = prompt_len // tt
    per = tt // HALO
    y_p = pl.pallas_call(
        functools.partial(_conv_prompt_kernel, tt=tt), grid=(n_prompt_seq, tiles),
        in_specs=[pl.BlockSpec((lt, HALO, LANES), lambda b, i: (0, jnp.maximum((b * tiles + i) * per - 1, 0), 0)),
                  pl.BlockSpec((lt, tt, LANES), lambda b, i: (0, b * tiles + i, 0))] + wspecs,
        out_specs=pl.BlockSpec((tt, d), lambda b, i: (b * tiles + i, 0)),
        out_shape=jax.ShapeDtypeStruct((tp, d), BF16),
        scratch_shapes=[pltpu.VMEM((lt, HALO + tt, LANES), F32), pltpu.VMEM((lt, tt, LANES), F32)],
        compiler_params=_params("parallel", "parallel"), name="conv_prompt")(glu_t, glu_t, *wargs)
    ts = sample_len
    sb = 16
    base = tp // (sb * ts)
    y_s = pl.pallas_call(
        functools.partial(_conv_sample_kernel, ts=ts, sb=sb), grid=(n_sample_seq // sb,),
        in_specs=[pl.BlockSpec((sb,) + hist.shape[1:], lambda b: (b, 0, 0)),
                  pl.BlockSpec((lt, sb * ts, LANES), lambda b: (0, base + b, 0))] + wspecs,
        out_specs=pl.BlockSpec((sb * ts, d), lambda b: (b, 0)),
        out_shape=jax.ShapeDtypeStruct((n_sample_seq * ts, d), BF16),
        scratch_shapes=[pltpu.VMEM((lt, sb * (HALO + ts), LANES), F32), pltpu.VMEM((lt, sb * ts, LANES), F32)],
        compiler_params=_params("parallel"), name="conv_sample")(hist, glu_t, *wargs)
    return jnp.concatenate([y_p, y_s], axis=0)


def _gmlp_mix_kernel(u_ref, v_ref, w_ref, bias_ref, lng_ref, lnb_ref, y_ref, vn_ref, *, n_prompt_chunks):
    vn = _layernorm(v_ref[...], lng_ref[...], lnb_ref[...])

    @pl.when(pl.program_id(0) >= n_prompt_chunks)
    def _():
        vn_ref[...] = vn

    vb = vn.astype(BF16)
    for g in range(N_SG):
        cols = slice(g * SG_DIM, (g + 1) * SG_DIM)
        mixed = jnp.dot(w_ref[0, g], vb[:, cols], preferred_element_type=F32) + bias_ref[0, :, cols]
        y_ref[:, cols] = (u_ref[:, cols] * mixed).astype(BF16)


def gmlp_mix(z, w_mix, bias_mix, ln_g, ln_b, n_prompt_chunks):
    t = z.shape[0]
    d = z.shape[1] // 2
    which = lambda c: (c >= n_prompt_chunks).astype(jnp.int32)
    return pl.pallas_call(
        functools.partial(_gmlp_mix_kernel, n_prompt_chunks=n_prompt_chunks), grid=(t // CHUNK,),
        in_specs=[pl.BlockSpec((CHUNK, d), lambda c: (c, 0)),
                  pl.BlockSpec((CHUNK, d), lambda c: (c, 1)),
                  pl.BlockSpec((1, N_SG, CHUNK, CHUNK), lambda c: (which(c), 0, 0, 0)),
                  pl.BlockSpec((1, CHUNK, d), lambda c: (which(c), 0, 0)),
                  pl.BlockSpec((1, d), lambda c: (0, 0)),
                  pl.BlockSpec((1, d), lambda c: (0, 0))],
        out_specs=[pl.BlockSpec((CHUNK, d), lambda c: (c, 0)),
                   pl.BlockSpec((CHUNK, d), lambda c: (jnp.maximum(c - n_prompt_chunks, 0), 0))],
        out_shape=[jax.ShapeDtypeStruct((t, d), BF16), jax.ShapeDtypeStruct((t - n_prompt_chunks * CHUNK, d), F32)],
        compiler_params=_params("arbitrary"), name="gmlp_mix")(
            z, z, w_mix, bias_mix, ln_g.reshape(1, d), ln_b.reshape(1, d))


def _ffn_kernel(blk_ref, exp_ref, nch_ref, x_ref, wg_ref, wu_ref, wd_ref, o_ref, a_ref, acc_ref, *,
                n_chunks, ch, nf, nk, tiled):
    b = pl.program_id(0)
    s = pl.program_id(1)
    nch = nch_ref[b]
    tf = wg_ref.shape[-1]
    tk = wd_ref.shape[1]
    kk = lax.rem(jnp.maximum(s - nf, 0), nk)

    def store_out(rows, val):
        o_ref[rows] = val.reshape((rows.size,) + o_ref.shape[1:]) if tiled else val

    def up_gate(rows):
        xr = x_ref[rows, :]
        g = jnp.dot(xr, wg_ref[0].astype(BF16), preferred_element_type=F32)
        u = jnp.dot(xr, wu_ref[0].astype(BF16), preferred_element_type=F32)
        a_ref[rows, pl.ds(pl.multiple_of(s * tf, tf), tf)] = (g * jax.nn.sigmoid(g) * u).astype(BF16)

    def down(rows):
        slab = pl.ds(pl.multiple_of(kk * tk, tk), tk)
        part = jnp.dot(a_ref[rows, slab], wd_ref[0].astype(BF16), preferred_element_type=F32)
        acc_ref[rows, :] = part + jnp.where(kk > 0, acc_ref[rows, :], 0.0)

        @pl.when(kk == nk - 1)
        def _():
            store_out(rows, acc_ref[rows, :])

    @pl.when(jnp.logical_and(b == 0, s == 0))
    def _():
        acc_ref[...] = jnp.zeros(acc_ref.shape, F32)

    def run(step_fn):
        @pl.when(nch == n_chunks)
        def _():
            step_fn(pl.ds(0, n_chunks * ch))

        @pl.when(jnp.logical_and(nch > 0, nch < n_chunks))
        def _():
            for c in range(n_chunks - 1):
                @pl.when(c < nch)
                def _():
                    step_fn(pl.ds(c * ch, ch))

    @pl.when(s < nf)
    def _():
        run(up_gate)

    @pl.when(s >= nf)
    def _():
        run(down)

    @pl.when(jnp.logical_and(s >= nf, kk == nk - 1))
    def _():
        for c in range(n_chunks):
            @pl.when(c >= nch)
            def _():
                store_out(pl.ds(c * ch, ch), jnp.zeros((ch, acc_ref.shape[-1]), F32))


def grouped_swiglu(x, wg, wu, wd, blk_idx, blk_exp, blk_nch, mblk, ch, tf, tk, tn, tiled):
    p, d = x.shape
    dff = wg.shape[-1]
    nb = blk_idx.shape[0]
    nf, nk, nn = dff // tf, dff // tk, d // tn
    up_tile = lambda b, s, nch: jnp.where(nch[b] > 0, jnp.minimum(s, nf - 1), nf - 1)
    down_step = lambda b, s, nch: jnp.where(nch[b] > 0, jnp.maximum(s - nf, 0), nn * nk - 1)
    out_tile = lambda s: jnp.maximum(s - nf, 0) // nk
    once = dict(pipeline_mode=pl.Buffered(1))
    x_spec = pl.BlockSpec((mblk, d), lambda b, s, blk, ex, nch: (blk[b], 0), **once)
    if tiled:
        out_spec = pl.BlockSpec((mblk, tn // LANES, LANES), lambda b, s, blk, ex, nch: (b, out_tile(s), 0), **once)
        out_shape = jax.ShapeDtypeStruct((p, d // LANES, LANES), F32)
    else:
        out_spec = pl.BlockSpec((mblk, tn), lambda b, s, blk, ex, nch: (b, out_tile(s)), **once)
        out_shape = jax.ShapeDtypeStruct((p, d), F32)
    grid_spec = pltpu.PrefetchScalarGridSpec(
        num_scalar_prefetch=3, grid=(nb, nf + nn * nk),
        in_specs=[x_spec,
                  pl.BlockSpec((1, d, tf), lambda b, s, blk, ex, nch: (ex[b], 0, up_tile(b, s, nch))),
                  pl.BlockSpec((1, d, tf), lambda b, s, blk, ex, nch: (ex[b], 0, up_tile(b, s, nch))),
                  pl.BlockSpec((1, tk, tn), lambda b, s, blk, ex, nch: (
                      ex[b], lax.rem(down_step(b, s, nch), nk), down_step(b, s, nch) // nk))],
        out_specs=out_spec,
        scratch_shapes=[pltpu.VMEM((mblk, dff), BF16), pltpu.VMEM((mblk, tn), F32)])
    return pl.pallas_call(
        functools.partial(_ffn_kernel, n_chunks=mblk // ch, ch=ch, nf=nf, nk=nk, tiled=tiled), grid_spec=grid_spec,
        out_shape=out_shape, compiler_params=_params("arbitrary", "arbitrary"),
        name="grouped_swiglu_tiled" if tiled else "grouped_swiglu")(blk_idx, blk_exp, blk_nch, x, wg, wu, wd)


def _gather_rows_kernel(src_ref, valid_ref, h_ref, o_ref, buf_ref, sem, *, rows):
    c = pl.program_id(0)
    last = pl.num_programs(0) - 1
    slot = lax.rem(c, 2)

    def request(j, slot_j):
        def body(i, carry):
            for u in range(ROW_DMA_UNROLL):
                r = i * ROW_DMA_UNROLL + u
                pltpu.make_async_copy(h_ref.at[pl.ds(src_ref[j * rows + r], 1)],
                                      buf_ref.at[slot_j, pl.ds(r, 1)], sem.at[slot_j]).start(priority=u % 2)
            return carry

        lax.fori_loop(0, rows // ROW_DMA_UNROLL, body, 0)

    @pl.when(jnp.logical_and(c == 0, valid_ref[0] > 0))
    def _():
        request(0, 0)

    nxt = jnp.minimum(c + 1, last)

    @pl.when(jnp.logical_and(c < last, valid_ref[nxt] > 0))
    def _():
        request(nxt, 1 - slot)

    @pl.when(valid_ref[c] > 0)
    def _():
        pltpu.make_async_copy(h_ref.at[pl.ds(0, rows)], buf_ref.at[slot], sem.at[slot]).wait()
        o_ref[...] = buf_ref[slot].reshape(o_ref.shape)

    @pl.when(valid_ref[c] == 0)
    def _():
        o_ref[...] = jnp.zeros(o_ref.shape, o_ref.dtype)


def gather_rows(h, src, valid, rows):
    p = src.shape[0]
    d = h.shape[1] * h.shape[2]
    grid_spec = pltpu.PrefetchScalarGridSpec(
        num_scalar_prefetch=2, grid=(p // rows,),
        in_specs=[pl.BlockSpec(memory_space=pl.ANY)],
        out_specs=pl.BlockSpec((rows, d), lambda c, src, valid: (c, 0)),
        scratch_shapes=[pltpu.VMEM((2, rows) + h.shape[1:], h.dtype), pltpu.SemaphoreType.DMA((2,))])
    return pl.pallas_call(
        functools.partial(_gather_rows_kernel, rows=rows), grid_spec=grid_spec,
        out_shape=jax.ShapeDtypeStruct((p, d), h.dtype),
        compiler_params=_params("arbitrary"), name="gather_rows")(src, valid, h)


def _combine_norm_kernel(pos_ref, x_ref, route_ref, g_ref, o_hbm, outp_ref, outs_ref, r_ref, sem, *, tc, n_first):
    i = pl.program_id(0)
    last = pl.num_programs(0) - 1
    slot = lax.rem(i, 2)

    def request(j, slot_j):
        def body(q, carry):
            for u in range(ROW_DMA_UNROLL // TOP_K):
                r = q * (ROW_DMA_UNROLL // TOP_K) + u
                for k in range(TOP_K):
                    pltpu.make_async_copy(o_hbm.at[pl.ds(pos_ref[TOP_K * (j * tc + r) + k], 1)],
                                          r_ref.at[slot_j, k, pl.ds(r, 1)], sem.at[slot_j]).start(priority=k % 2)
            return carry

        lax.fori_loop(0, tc * TOP_K // ROW_DMA_UNROLL, body, 0)

    @pl.when(i == 0)
    def _():
        request(0, 0)

    @pl.when(i < last)
    def _():
        request(i + 1, 1 - slot)

    for k in range(TOP_K):
        pltpu.make_async_copy(o_hbm.at[pl.ds(0, tc)], r_ref.at[slot, k], sem.at[slot]).wait()
    route = route_ref[...]
    g1 = route[:, 2:3]
    g2 = route[:, 3:4]
    d = x_ref.shape[-1]
    y = x_ref[...] + (g1 * r_ref[slot, 0].reshape(tc, d) + g2 * r_ref[slot, 1].reshape(tc, d))
    res = _rms(y, g_ref[...])

    @pl.when(i < n_first)
    def _():
        outp_ref[...] = res

    @pl.when(i >= n_first)
    def _():
        outs_ref[...] = res


def combine_norm(x, route, pos, expert_out, g, t_first, tc=256):
    t, d = x.shape
    tile = expert_out.shape[1:]
    n_first = t_first // tc
    grid_spec = pltpu.PrefetchScalarGridSpec(
        num_scalar_prefetch=1, grid=(t // tc,),
        in_specs=[pl.BlockSpec((tc, d), lambda i, pos: (i, 0)),
                  pl.BlockSpec((tc, LANES), lambda i, pos: (i, 0)),
                  pl.BlockSpec((1, d), lambda i, pos: (0, 0)),
                  pl.BlockSpec(memory_space=pl.ANY)],
        out_specs=[pl.BlockSpec((tc, d), lambda i, pos: (jnp.minimum(i, n_first - 1), 0)),
                   pl.BlockSpec((tc, d), lambda i, pos: (jnp.maximum(i - n_first, 0), 0))],
        scratch_shapes=[pltpu.VMEM((2, TOP_K, tc) + tile, F32), pltpu.SemaphoreType.DMA((2,))])
    return pl.pallas_call(
        functools.partial(_combine_norm_kernel, tc=tc, n_first=n_first), grid_spec=grid_spec,
        out_shape=[jax.ShapeDtypeStruct((t_first, d), F32), jax.ShapeDtypeStruct((t - t_first, d), F32)],
        compiler_params=_params("arbitrary"), name="combine_norm")(pos, x, route, g.reshape(1, d), expert_out)


def _dispatch_plan(e_slot, n_exp, mblk, ch, n_blocks):
    n_slots = e_slot.shape[0]
    onehot = (e_slot[:, None] == jnp.arange(n_exp, dtype=jnp.int32)[None, :]).astype(jnp.int32)
    seg = 512
    oh = onehot.astype(F32).reshape(n_slots // seg, seg, n_exp)
    within = jnp.einsum("ts,bse->bte", jnp.tril(jnp.ones((seg, seg), F32)), oh, precision=lax.Precision.HIGHEST)
    totals = within[:, -1, :]
    csum = (within + (jnp.cumsum(totals, axis=0) - totals)[:, None, :]).reshape(n_slots, n_exp).astype(jnp.int32)
    rank = jnp.sum(csum * onehot, axis=1) - 1
    counts = csum[-1]
    nblk = (counts + mblk - 1) // mblk
    blk_end = jnp.cumsum(nblk)
    blk_start = blk_end - nblk
    dest = blk_start[e_slot] * mblk + rank
    n_used = blk_end[-1]
    step = jnp.arange(n_blocks, dtype=jnp.int32)
    blk_idx = jnp.minimum(step, n_used - 1)
    blk_exp = jnp.minimum(jnp.sum((blk_idx[:, None] >= blk_end[None, :]).astype(jnp.int32), axis=1), n_exp - 1)
    rows_in = jnp.clip(counts[blk_exp] - (blk_idx - blk_start[blk_exp]) * mblk, 0, mblk)
    blk_nch = jnp.where(step < n_used, (rows_in + ch - 1) // ch, 0).astype(jnp.int32)
    src = jnp.zeros((n_blocks * mblk,), jnp.int32).at[dest].set(jnp.arange(n_slots, dtype=jnp.int32) // TOP_K)
    cpb = mblk // ch
    chunk_valid = (jnp.arange(cpb, dtype=jnp.int32)[None, :] < blk_nch[:, None]).astype(jnp.int32).reshape(-1)
    return dest.astype(jnp.int32), src, blk_idx.astype(jnp.int32), blk_exp.astype(jnp.int32), blk_nch, chunk_valid


def kernel(x_prompt, x_sample, state_conv, norm_mix, norm_ffn, norm_final, conv_w_in, conv_b_in, conv_w_dw, conv_b_dw, conv_ln_g, conv_ln_b, conv_w_out, conv_b_out, gmlp_w_in, gmlp_b_in, gmlp_ln_g, gmlp_ln_b, gmlp_w_s, gmlp_b_s, gmlp_w_out, gmlp_b_out, ffn_w_gate, ffn_w_up, ffn_w_down, moe_router, moe_w_gate, moe_w_up, moe_w_down):
    nb_p, len_p, d = x_prompt.shape
    nb_s, len_s, _ = x_sample.shape
    tp, ts = nb_p * len_p, nb_s * len_s
    t = tp + ts
    hist_rows = CONV_W - 1

    x, h = concat_rmsnorm(x_prompt.reshape(tp, d), x_sample.reshape(ts, d), norm_mix[0])
    glu_t = matmul_glu(h, conv_w_in[0], conv_b_in[0])
    y = conv_ln_silu(glu_t, state_conv[0], conv_w_dw[0], conv_b_dw[0], conv_ln_g[0], conv_ln_b[0],
                     nb_p, len_p, nb_s, len_s)
    x, h = matmul_residual(y, conv_w_out[0], conv_b_out[0], x, norm_ffn[0])
    tails = jnp.stack([glu_t[:, (b + 1) * len_p - hist_rows:(b + 1) * len_p] for b in range(nb_p)])
    new_conv_prompt = tails.transpose(0, 2, 1, 3).reshape(nb_p, hist_rows, d)[None]
    glu_s = glu_t[:, tp:].reshape(LANE_TILES, nb_s, len_s, LANES).transpose(1, 2, 0, 3).reshape(nb_s, len_s, d)
    new_conv_sample = jnp.concatenate([state_conv[0], glu_s], axis=1)[:, len_s:][None]

    mblk0, ch0 = 1152, 576
    nb0 = t // mblk0
    ffn = grouped_swiglu(h, ffn_w_gate, ffn_w_up, ffn_w_down,
                         jnp.arange(nb0, dtype=jnp.int32), jnp.zeros((nb0,), jnp.int32),
                         jnp.full((nb0,), mblk0 // ch0, jnp.int32), mblk0, ch0, FFN_TF, FFN_TK, FFN_TN, False)

    x, h = add_rmsnorm(x, ffn, norm_mix[1])
    z = matmul_gelu(h, gmlp_w_in[0], gmlp_b_in[0])
    tril = jnp.tril(jnp.ones((CHUNK, CHUNK), dtype=bool))
    w_prompt = jnp.where(tril[None], gmlp_w_s[0], 0)
    seqs = CHUNK // len_s
    w_small = jnp.where(tril[None, :len_s, :len_s], gmlp_w_s[0][:, :len_s, :len_s], 0)
    w_sample = jnp.einsum("ab,gts->gatbs", jnp.eye(seqs, dtype=F32), w_small).reshape(N_SG, CHUNK, CHUNK)
    w_mix = jnp.stack([w_prompt, w_sample]).astype(BF16)
    bias_rows = jnp.stack([gmlp_b_s[0].T, jnp.tile(gmlp_b_s[0].T[:len_s], (seqs, 1))])
    bias_mix = jnp.repeat(bias_rows, SG_DIM, axis=2)
    y, v = gmlp_mix(z, w_mix, bias_mix, gmlp_ln_g[0], gmlp_ln_b[0], tp // CHUNK)
    x = matmul_residual(y, gmlp_w_out[0], gmlp_b_out[0], x)
    new_chunk_v_sample = v.reshape(nb_s, len_s, d)[None]

    h_t, route = norm_route(x, norm_ffn[1], moe_router[0])
    e_slot = route[:, :TOP_K].astype(jnp.int32).reshape(-1)
    mblk1, ch1 = 1216, 608
    n_blocks = (t * TOP_K) // mblk1 + N_EXP
    dest, src, blk_idx, blk_exp, blk_nch, chunk_valid = _dispatch_plan(e_slot, N_EXP, mblk1, ch1, n_blocks)
    xs = gather_rows(h_t, src, chunk_valid, ch1)
    eo = grouped_swiglu(xs, moe_w_gate[0], moe_w_up[0], moe_w_down[0], blk_idx, blk_exp, blk_nch, mblk1, ch1,
                        FFN_TF, FFN_TK, FFN_TN, True)
    out_p, out_s = combine_norm(x, route, dest, eo, norm_final, tp)
    return (out_p.reshape(nb_p, len_p, d), out_s.reshape(nb_s, len_s, d), new_conv_prompt, new_conv_sample,
            new_chunk_v_sample)
```

```python
import functools

import jax
import jax.numpy as jnp
from jax import lax
from jax.experimental import pallas as pl
from jax.experimental.pallas import tpu as pltpu

F32 = jnp.float32
BF16 = jnp.bfloat16

D_MODEL = 2048
CONV_W = 31
CHUNK = 128
N_SG = 8
SG_DIM = D_MODEL // N_SG
N_EXP = 8
TOP_K = 2
EPS_RMS = 1e-6
EPS_LN = 1e-5

V7X_VMEM_BYTES = 64 * 1024 * 1024
VMEM_LIMIT = V7X_VMEM_BYTES - 8 * 1024 * 1024
LANES = 128
LANE_TILES = D_MODEL // LANES
HALO = 32
FFN_TF, FFN_TK, FFN_TN = 256, 1024, 1024
ROW_DMA_UNROLL = 8


def _params(*sem):
    return pltpu.CompilerParams(dimension_semantics=sem, vmem_limit_bytes=VMEM_LIMIT)


def _rms(x, g):
    return x * lax.rsqrt(jnp.mean(x * x, axis=-1, keepdims=True) + EPS_RMS) * g


def _layernorm(x, g, b):
    mu = jnp.mean(x, axis=-1, keepdims=True)
    xc = x - mu
    return xc * lax.rsqrt(jnp.mean(xc * xc, axis=-1, keepdims=True) + EPS_LN) * g + b


def _concat_norm_kernel(xa_ref, xb_ref, g_ref, xo_ref, h_ref, *, n_first):
    def emit(x):
        xo_ref[...] = x
        h_ref[...] = _rms(x, g_ref[...]).astype(BF16)

    @pl.when(pl.program_id(0) < n_first)
    def _():
        emit(xa_ref[...])

    @pl.when(pl.program_id(0) >= n_first)
    def _():
        emit(xb_ref[...])


def _add_norm_kernel(x_ref, d_ref, g_ref, xo_ref, h_ref):
    x = x_ref[...] + d_ref[...]
    xo_ref[...] = x
    h_ref[...] = _rms(x, g_ref[...]).astype(BF16)


def concat_rmsnorm(xa, xb, g, tr=512):
    ta, d = xa.shape
    t = ta + xb.shape[0]
    n_first = ta // tr
    row = pl.BlockSpec((tr, d), lambda i: (i, 0))
    return pl.pallas_call(
        functools.partial(_concat_norm_kernel, n_first=n_first), grid=(t // tr,),
        in_specs=[pl.BlockSpec((tr, d), lambda i: (jnp.minimum(i, n_first - 1), 0)),
                  pl.BlockSpec((tr, d), lambda i: (jnp.maximum(i - n_first, 0), 0)),
                  pl.BlockSpec((1, d), lambda i: (0, 0))],
        out_specs=[row, row],
        out_shape=[jax.ShapeDtypeStruct((t, d), F32), jax.ShapeDtypeStruct((t, d), BF16)],
        compiler_params=_params("arbitrary"), name="concat_rmsnorm")(xa, xb, g.reshape(1, d))


def add_rmsnorm(x, delta, g, tr=512):
    t, d = x.shape
    row = pl.BlockSpec((tr, d), lambda i: (i, 0))
    return pl.pallas_call(
        _add_norm_kernel, grid=(t // tr,), in_specs=[row, row, pl.BlockSpec((1, d), lambda i: (0, 0))],
        out_specs=[row, row],
        out_shape=[jax.ShapeDtypeStruct((t, d), F32), jax.ShapeDtypeStruct((t, d), BF16)],
        compiler_params=_params("parallel"), name="add_rmsnorm")(x, delta, g.reshape(1, d))


def _gelu_exact(z):
    return 0.5 * z * (1.0 + lax.erf(z * (0.5 ** 0.5)))


def _cast_on_first_row_tile(w_ref, ws_ref):
    @pl.when(pl.program_id(1) == 0)
    def _():
        ws_ref[...] = w_ref[...].astype(BF16)


def _mm_glu_kernel(a_ref, wa_ref, wg_ref, ba_ref, bg_ref, o_ref, was_ref, wgs_ref):
    _cast_on_first_row_tile(wa_ref, was_ref)
    _cast_on_first_row_tile(wg_ref, wgs_ref)
    act = a_ref[...]
    a = jnp.dot(act, was_ref[...], preferred_element_type=F32) + ba_ref[...]
    gate = jnp.dot(act, wgs_ref[...], preferred_element_type=F32) + bg_ref[...]
    glu = a * jax.nn.sigmoid(gate)
    for j in range(o_ref.shape[0]):
        o_ref[j] = glu[:, j * LANES:(j + 1) * LANES]


def _mm_gelu_kernel(a_ref, w_ref, b_ref, o_ref, ws_ref):
    _cast_on_first_row_tile(w_ref, ws_ref)
    z = jnp.dot(a_ref[...], ws_ref[...], preferred_element_type=F32) + b_ref[...]
    o_ref[...] = _gelu_exact(z)


def _top2_route(h, r_ref):
    r = r_ref[...]
    h_hi = h.astype(BF16)
    h_lo = (h - h_hi.astype(F32)).astype(BF16)
    r_hi = r.astype(BF16)
    r_lo = (r - r_hi.astype(F32)).astype(BF16)
    logits = (jnp.dot(h_hi, r_hi, preferred_element_type=F32) + jnp.dot(h_hi, r_lo, preferred_element_type=F32)
              + jnp.dot(h_lo, r_hi, preferred_element_type=F32))
    lane = lax.broadcasted_iota(jnp.int32, logits.shape, 1).astype(F32)
    neg = jnp.float32(-jnp.inf)
    logits = jnp.where(lane < N_EXP, logits, neg)
    m1 = jnp.max(logits, axis=-1, keepdims=True)
    i1 = jnp.min(jnp.where(logits == m1, lane, float(LANES)), axis=-1, keepdims=True)
    rest = jnp.where(lane == i1, neg, logits)
    m2 = jnp.max(rest, axis=-1, keepdims=True)
    i2 = jnp.min(jnp.where(rest == m2, lane, float(LANES)), axis=-1, keepdims=True)
    e2 = jnp.exp(m2 - m1)
    g1 = 1.0 / (1.0 + e2)
    g2 = e2 / (1.0 + e2)
    route = jnp.where(lane == 0, i1, 0.0)
    route = jnp.where(lane == 1, i2, route)
    route = jnp.where(lane == 2, g1, route)
    return jnp.where(lane == 3, g2, route)


def _mm_res_kernel(a_ref, w_ref, b_ref, r_ref, *rest, norm):
    if norm:
        g_ref, x_ref, h_ref, ws_ref = rest
    else:
        x_ref, ws_ref = rest

    @pl.when(pl.program_id(0) == 0)
    def _():
        ws_ref[...] = w_ref[...].astype(BF16)

    x = r_ref[...] + (jnp.dot(a_ref[...], ws_ref[...], preferred_element_type=F32) + b_ref[...])
    x_ref[...] = x
    if norm:
        h_ref[...] = _rms(x, g_ref[...]).astype(BF16)


def _norm_route_kernel(x_ref, g_ref, rt_ref, h_ref, route_ref):
    h = _rms(x_ref[...], g_ref[...])
    h_ref[...] = h.astype(BF16).reshape(h_ref.shape)
    route_ref[...] = _top2_route(h, rt_ref)


def matmul_glu(act, w, b, tm=1024, tn=512):
    t, k = act.shape
    n = w.shape[1] // 2
    nb = n // tn
    b = b.reshape(1, 2 * n)
    return pl.pallas_call(
        _mm_glu_kernel, grid=(nb, t // tm),
        in_specs=[pl.BlockSpec((tm, k), lambda j, i: (i, 0)),
                  pl.BlockSpec((k, tn), lambda j, i: (0, j)),
                  pl.BlockSpec((k, tn), lambda j, i: (0, j + nb)),
                  pl.BlockSpec((1, tn), lambda j, i: (0, j)),
                  pl.BlockSpec((1, tn), lambda j, i: (0, j + nb))],
        out_specs=pl.BlockSpec((tn // LANES, tm, LANES), lambda j, i: (j, i, 0)),
        out_shape=jax.ShapeDtypeStruct((n // LANES, t, LANES), F32),
        scratch_shapes=[pltpu.VMEM((k, tn), BF16), pltpu.VMEM((k, tn), BF16)],
        compiler_params=_params("parallel", "arbitrary"), name="matmul_glu")(act, w, w, b, b)


def matmul_gelu(act, w, b, tm=1024, tn=1024):
    t, k = act.shape
    n = w.shape[1]
    return pl.pallas_call(
        _mm_gelu_kernel, grid=(n // tn, t // tm),
        in_specs=[pl.BlockSpec((tm, k), lambda j, i: (i, 0)),
                  pl.BlockSpec((k, tn), lambda j, i: (0, j)),
                  pl.BlockSpec((1, tn), lambda j, i: (0, j))],
        out_specs=pl.BlockSpec((tm, tn), lambda j, i: (i, j)),
        out_shape=jax.ShapeDtypeStruct((t, n), F32),
        scratch_shapes=[pltpu.VMEM((k, tn), BF16)],
        compiler_params=_params("parallel", "arbitrary"), name="matmul_gelu")(act, w, b.reshape(1, n))


def matmul_residual(act, w, b, res, g=None, tm=512):
    t, k = act.shape
    n = w.shape[1]
    norm = g is not None
    row = lambda width: pl.BlockSpec((tm, width), lambda i: (i, 0))
    fixed = lambda shape: pl.BlockSpec(shape, lambda i: (0, 0))
    in_specs = [row(k), pl.BlockSpec((k, n), lambda i: (0, 0), pipeline_mode=pl.Buffered(1)), fixed((1, n)), row(n)]
    args = [act, w, b.reshape(1, n), res]
    out_specs, out_shape = row(n), jax.ShapeDtypeStruct((t, n), F32)
    if norm:
        in_specs.append(fixed((1, n)))
        args.append(g.reshape(1, n))
        out_specs, out_shape = [out_specs, row(n)], [out_shape, jax.ShapeDtypeStruct((t, n), BF16)]
    return pl.pallas_call(
        functools.partial(_mm_res_kernel, norm=norm), grid=(t // tm,),
        in_specs=in_specs, out_specs=out_specs, out_shape=out_shape,
        scratch_shapes=[pltpu.VMEM((k, n), BF16)],
        compiler_params=_params("arbitrary"),
        name="matmul_residual_norm" if norm else "matmul_residual")(*args)


def norm_route(x, g, router, tr=512):
    t, d = x.shape
    r = jnp.zeros((d, LANES), F32).at[:, :N_EXP].set(router)
    return pl.pallas_call(
        _norm_route_kernel, grid=(t // tr,),
        in_specs=[pl.BlockSpec((tr, d), lambda i: (i, 0)), pl.BlockSpec((1, d), lambda i: (0, 0)),
                  pl.BlockSpec((d, LANES), lambda i: (0, 0))],
        out_specs=[pl.BlockSpec((tr, d // LANES, LANES), lambda i: (i, 0, 0)),
                   pl.BlockSpec((tr, LANES), lambda i: (i, 0))],
        out_shape=[jax.ShapeDtypeStruct((t, d // LANES, LANES), BF16), jax.ShapeDtypeStruct((t, LANES), F32)],
        compiler_params=_params("parallel"), name="norm_route")(x, g.reshape(1, d), r)


def _conv_taps(xp_ref, y_ref, wdw_ref, bdw_ref, l, x0, y0, tt, rows):
    first = HALO - (CONV_W - 1)
    for r0 in range(0, tt, rows):
        acc = jnp.broadcast_to(bdw_ref[l], (rows, LANES))
        for k in range(CONV_W):
            lo = x0 + r0 + first + k
            acc = acc + xp_ref[l, lo:lo + rows, :] * wdw_ref[l, k:k + 1, :]
        y_ref[l, y0 + r0:y0 + r0 + rows, :] = acc


def _ln_silu_store(y_ref, lng_ref, lnb_ref, o_ref):
    n = y_ref.shape[1]
    d = LANE_TILES * LANES
    s1 = y_ref[0]
    for l in range(1, LANE_TILES):
        s1 = s1 + y_ref[l]
    mu = jnp.sum(s1, axis=-1, keepdims=True) * (1.0 / d)
    s2 = jnp.zeros((n, LANES), F32)
    for l in range(LANE_TILES):
        c = y_ref[l] - mu
        s2 = s2 + c * c
    inv = lax.rsqrt(jnp.sum(s2, axis=-1, keepdims=True) * (1.0 / d) + EPS_LN)
    for l in range(LANE_TILES):
        y = (y_ref[l] - mu) * inv * lng_ref[l] + lnb_ref[l]
        o_ref[:, l * LANES:(l + 1) * LANES] = (y * jax.nn.sigmoid(y)).astype(BF16)


def _conv_prompt_kernel(halo_ref, main_ref, wdw_ref, bdw_ref, lng_ref, lnb_ref, o_ref, xp_ref, y_ref, *, tt):
    xp_ref[:, 0:HALO, :] = jnp.where(pl.program_id(1) > 0, halo_ref[...], 0.0)
    xp_ref[:, HALO:HALO + tt, :] = main_ref[...]

    def lane_tile(l, carry):
        _conv_taps(xp_ref, y_ref, wdw_ref, bdw_ref, l, 0, 0, tt, 32)
        return carry

    lax.fori_loop(0, LANE_TILES, lane_tile, 0)
    _ln_silu_store(y_ref, lng_ref, lnb_ref, o_ref)


def _conv_sample_kernel(hist_ref, main_ref, wdw_ref, bdw_ref, lng_ref, lnb_ref, o_ref, xp_ref, y_ref, *, ts, sb):
    span = HALO + ts
    pad = HALO - hist_ref.shape[1]
    for s in range(sb):
        xp_ref[:, s * span:s * span + pad, :] = jnp.zeros((LANE_TILES, pad, LANES), F32)
        for l in range(LANE_TILES):
            xp_ref[l, s * span + pad:s * span + HALO, :] = hist_ref[s, :, l * LANES:(l + 1) * LANES]
        xp_ref[:, s * span + HALO:(s + 1) * span, :] = main_ref[:, s * ts:(s + 1) * ts, :]

    def lane_tile(l, carry):
        for s in range(sb):
            _conv_taps(xp_ref, y_ref, wdw_ref, bdw_ref, l, s * span, s * ts, ts, ts)
        return carry

    lax.fori_loop(0, LANE_TILES, lane_tile, 0)
    _ln_silu_store(y_ref, lng_ref, lnb_ref, o_ref)


def _lane_tiled(v):
    return v.reshape(v.shape[0], LANE_TILES, LANES).transpose(1, 0, 2)


def conv_ln_silu(glu_t, hist, w_dw, b_dw, ln_g, ln_b, n_prompt_seq, prompt_len, n_sample_seq, sample_len, tt=256):
    lt, t, _ = glu_t.shape
    d = lt * LANES
    tp = n_prompt_seq * prompt_len
    z3 = lambda *_: (0, 0, 0)
    wspecs = [pl.BlockSpec((lt, CONV_W, LANES), z3), pl.BlockSpec((lt, 1, LANES), z3),
              pl.BlockSpec((lt, 1, LANES), z3), pl.BlockSpec((lt, 1, LANES), z3)]
    wargs = (_lane_tiled(w_dw), _lane_tiled(b_dw.reshape(1, d)), _lane_tiled(ln_g.reshape(1, d)),
             _lane_tiled(ln_b.reshape(1, d)))
    tiles = prompt_len // tt
    per = tt // HALO
    y_p = pl.pallas_call(
        functools.partial(_conv_prompt_kernel, tt=tt), grid=(n_prompt_seq, tiles),
        in_specs=[pl.BlockSpec((lt, HALO, LANES), lambda b, i: (0, jnp.maximum((b * tiles + i) * per - 1, 0), 0)),
                  pl.BlockSpec((lt, tt, LANES), lambda b, i: (0, b * tiles + i, 0))] + wspecs,
        out_specs=pl.BlockSpec((tt, d), lambda b, i: (b * tiles + i, 0)),
        out_shape=jax.ShapeDtypeStruct((tp, d), BF16),
        scratch_shapes=[pltpu.VMEM((lt, HALO + tt, LANES), F32), pltpu.VMEM((lt, tt, LANES), F32)],
        compiler_params=_params("parallel", "parallel"), name="conv_prompt")(glu_t, glu_t, *wargs)
    ts = sample_len
    sb = 16
    base = tp // (sb * ts)
    y_s = pl.pallas_call(
        functools.partial(_conv_sample_kernel, ts=ts, sb=sb), grid=(n_sample_seq // sb,),
        in_specs=[pl.BlockSpec((sb,) + hist.shape[1:], lambda b: (b, 0, 0)),
                  pl.BlockSpec((lt, sb * ts, LANES), lambda b: (0, base + b, 0))] + wspecs,
        out_specs=pl.BlockSpec((sb * ts, d), lambda b: (b, 0)),
        out_shape=jax.ShapeDtypeStruct((n_sample_seq * ts, d), BF16),
        scratch_shapes=[pltpu.VMEM((lt, sb * (HALO + ts), LANES), F32), pltpu.VMEM((lt, sb * ts, LANES), F32)],
        compiler_params=_params("parallel"), name="conv_sample")(hist, glu_t, *wargs)
    return jnp.concatenate([y_p, y_s], axis=0)


def _gmlp_mix_kernel(u_ref, v_ref, w_ref, bias_ref, lng_ref, lnb_ref, y_ref, vn_ref, *, n_prompt_chunks):
    vn = _layernorm(v_ref[...], lng_ref[...], lnb_ref[...])

    @pl.when(pl.program_id(0) >= n_prompt_chunks)
    def _():
        vn_ref[...] = vn

    vb = vn.astype(BF16)
    for g in range(N_SG):
        cols = slice(g * SG_DIM, (g + 1) * SG_DIM)
        mixed = jnp.dot(w_ref[0, g], vb[:, cols], preferred_element_type=F32) + bias_ref[0, :, cols]
        y_ref[:, cols] = (u_ref[:, cols] * mixed).astype(BF16)


def gmlp_mix(z, w_mix, bias_mix, ln_g, ln_b, n_prompt_chunks):
    t = z.shape[0]
    d = z.shape[1] // 2
    which = lambda c: (c >= n_prompt_chunks).astype(jnp.int32)
    return pl.pallas_call(
        functools.partial(_gmlp_mix_kernel, n_prompt_chunks=n_prompt_chunks), grid=(t // CHUNK,),
        in_specs=[pl.BlockSpec((CHUNK, d), lambda c: (c, 0)),
                  pl.BlockSpec((CHUNK, d), lambda c: (c, 1)),
                  pl.BlockSpec((1, N_SG, CHUNK, CHUNK), lambda c: (which(c), 0, 0, 0)),
                  pl.BlockSpec((1, CHUNK, d), lambda c: (which(c), 0, 0)),
                  pl.BlockSpec((1, d), lambda c: (0, 0)),
                  pl.BlockSpec((1, d), lambda c: (0, 0))],
        out_specs=[pl.BlockSpec((CHUNK, d), lambda c: (c, 0)),
                   pl.BlockSpec((CHUNK, d), lambda c: (jnp.maximum(c - n_prompt_chunks, 0), 0))],
        out_shape=[jax.ShapeDtypeStruct((t, d), BF16), jax.ShapeDtypeStruct((t - n_prompt_chunks * CHUNK, d), F32)],
        compiler_params=_params("arbitrary"), name="gmlp_mix")(
            z, z, w_mix, bias_mix, ln_g.reshape(1, d), ln_b.reshape(1, d))


def _ffn_kernel(blk_ref, exp_ref, nch_ref, x_ref, wg_ref, wu_ref, wd_ref, o_ref, a_ref, acc_ref, *,
                n_chunks, ch, nf, nk, tiled):
    b = pl.program_id(0)
    s = pl.program_id(1)
    nch = nch_ref[b]
    tf = wg_ref.shape[-1]
    tk = wd_ref.shape[1]
    kk = lax.rem(jnp.maximum(s - nf, 0), nk)

    def store_out(rows, val):
        o_ref[rows] = val.reshape((rows.size,) + o_ref.shape[1:]) if tiled else val

    def up_gate(rows):
        xr = x_ref[rows, :]
        g = jnp.dot(xr, wg_ref[0].astype(BF16), preferred_element_type=F32)
        u = jnp.dot(xr, wu_ref[0].astype(BF16), preferred_element_type=F32)
        a_ref[rows, pl.ds(pl.multiple_of(s * tf, tf), tf)] = (g * jax.nn.sigmoid(g) * u).astype(BF16)

    def down(rows):
        slab = pl.ds(pl.multiple_of(kk * tk, tk), tk)
        part = jnp.dot(a_ref[rows, slab], wd_ref[0].astype(BF16), preferred_element_type=F32)
        acc_ref[rows, :] = part + jnp.where(kk > 0, acc_ref[rows, :], 0.0)

        @pl.when(kk == nk - 1)
        def _():
            store_out(rows, acc_ref[rows, :])

    @pl.when(jnp.logical_and(b == 0, s == 0))
    def _():
        acc_ref[...] = jnp.zeros(acc_ref.shape, F32)

    def run(step_fn):
        @pl.when(nch == n_chunks)
        def _():
            step_fn(pl.ds(0, n_chunks * ch))

        @pl.when(jnp.logical_and(nch > 0, nch < n_chunks))
        def _():
            for c in range(n_chunks - 1):
                @pl.when(c < nch)
                def _():
                    step_fn(pl.ds(c * ch, ch))

    @pl.when(s < nf)
    def _():
        run(up_gate)

    @pl.when(s >= nf)
    def _():
        run(down)

    @pl.when(jnp.logical_and(s >= nf, kk == nk - 1))
    def _():
        for c in range(n_chunks):
            @pl.when(c >= nch)
            def _():
                store_out(pl.ds(c * ch, ch), jnp.zeros((ch, acc_ref.shape[-1]), F32))


def grouped_swiglu(x, wg, wu, wd, blk_idx, blk_exp, blk_nch, mblk, ch, tf, tk, tn, tiled):
    p, d = x.shape
    dff = wg.shape[-1]
    nb = blk_idx.shape[0]
    nf, nk, nn = dff // tf, dff // tk, d // tn
    up_tile = lambda b, s, nch: jnp.where(nch[b] > 0, jnp.minimum(s, nf - 1), nf - 1)
    down_step = lambda b, s, nch: jnp.where(nch[b] > 0, jnp.maximum(s - nf, 0), nn * nk - 1)
    out_tile = lambda s: jnp.maximum(s - nf, 0) // nk
    once = dict(pipeline_mode=pl.Buffered(1))
    x_spec = pl.BlockSpec((mblk, d), lambda b, s, blk, ex, nch: (blk[b], 0), **once)
    if tiled:
        out_spec = pl.BlockSpec((mblk, tn // LANES, LANES), lambda b, s, blk, ex, nch: (b, out_tile(s), 0), **once)
        out_shape = jax.ShapeDtypeStruct((p, d // LANES, LANES), F32)
    else:
        out_spec = pl.BlockSpec((mblk, tn), lambda b, s, blk, ex, nch: (b, out_tile(s)), **once)
        out_shape = jax.ShapeDtypeStruct((p, d), F32)
    grid_spec = pltpu.PrefetchScalarGridSpec(
        num_scalar_prefetch=3, grid=(nb, nf + nn * nk),
        in_specs=[x_spec,
                  pl.BlockSpec((1, d, tf), lambda b, s, blk, ex, nch: (ex[b], 0, up_tile(b, s, nch))),
                  pl.BlockSpec((1, d, tf), lambda b, s, blk, ex, nch: (ex[b], 0, up_tile(b, s, nch))),
                  pl.BlockSpec((1, tk, tn), lambda b, s, blk, ex, nch: (
                      ex[b], lax.rem(down_step(b, s, nch), nk), down_step(b, s, nch) // nk))],
        out_specs=out_spec,
        scratch_shapes=[pltpu.VMEM((mblk, dff), BF16), pltpu.VMEM((mblk, tn), F32)])
    return pl.pallas_call(
        functools.partial(_ffn_kernel, n_chunks=mblk // ch, ch=ch, nf=nf, nk=nk, tiled=tiled), grid_spec=grid_spec,
        out_shape=out_shape, compiler_params=_params("arbitrary", "arbitrary"),
        name="grouped_swiglu_tiled" if tiled else "grouped_swiglu")(blk_idx, blk_exp, blk_nch, x, wg, wu, wd)


def _gather_rows_kernel(src_ref, valid_ref, h_ref, o_ref, buf_ref, sem, *, rows):
    c = pl.program_id(0)
    last = pl.num_programs(0) - 1
    slot = lax.rem(c, 2)

    def request(j, slot_j):
        def body(i, carry):
            for u in range(ROW_DMA_UNROLL):
                r = i * ROW_DMA_UNROLL + u
                pltpu.make_async_copy(h_ref.at[pl.ds(src_ref[j * rows + r], 1)],
                                      buf_ref.at[slot_j, pl.ds(r, 1)], sem.at[slot_j]).start(priority=u % 2)
            return carry

        lax.fori_loop(0, rows // ROW_DMA_UNROLL, body, 0)

    @pl.when(jnp.logical_and(c == 0, valid_ref[0] > 0))
    def _():
        request(0, 0)

    nxt = jnp.minimum(c + 1, last)

    @pl.when(jnp.logical_and(c < last, valid_ref[nxt] > 0))
    def _():
        request(nxt, 1 - slot)

    @pl.when(valid_ref[c] > 0)
    def _():
        pltpu.make_async_copy(h_ref.at[pl.ds(0, rows)], buf_ref.at[slot], sem.at[slot]).wait()
        o_ref[...] = buf_ref[slot].reshape(o_ref.shape)

    @pl.when(valid_ref[c] == 0)
    def _():
        o_ref[...] = jnp.zeros(o_ref.shape, o_ref.dtype)


def gather_rows(h, src, valid, rows):
    p = src.shape[0]
    d = h.shape[1] * h.shape[2]
    grid_spec = pltpu.PrefetchScalarGridSpec(
        num_scalar_prefetch=2, grid=(p // rows,),
        in_specs=[pl.BlockSpec(memory_space=pl.ANY)],
        out_specs=pl.BlockSpec((rows, d), lambda c, src, valid: (c, 0)),
        scratch_shapes=[pltpu.VMEM((2, rows) + h.shape[1:], h.dtype), pltpu.SemaphoreType.DMA((2,))])
    return pl.pallas_call(
        functools.partial(_gather_rows_kernel, rows=rows), grid_spec=grid_spec,
        out_shape=jax.ShapeDtypeStruct((p, d), h.dtype),
        compiler_params=_params("arbitrary"), name="gather_rows")(src, valid, h)


def _combine_norm_kernel(pos_ref, x_ref, route_ref, g_ref, o_hbm, outp_ref, outs_ref, r_ref, sem, *, tc, n_first):
    i = pl.program_id(0)
    last = pl.num_programs(0) - 1
    slot = lax.rem(i, 2)

    def request(j, slot_j):
        def body(q, carry):
            for u in range(ROW_DMA_UNROLL // TOP_K):
                r = q * (ROW_DMA_UNROLL // TOP_K) + u
                for k in range(TOP_K):
                    pltpu.make_async_copy(o_hbm.at[pl.ds(pos_ref[TOP_K * (j * tc + r) + k], 1)],
                                          r_ref.at[slot_j, k, pl.ds(r, 1)], sem.at[slot_j]).start(priority=k % 2)
            return carry

        lax.fori_loop(0, tc * TOP_K // ROW_DMA_UNROLL, body, 0)

    @pl.when(i == 0)
    def _():
        request(0, 0)

    @pl.when(i < last)
    def _():
        request(i + 1, 1 - slot)

    for k in range(TOP_K):
        pltpu.make_async_copy(o_hbm.at[pl.ds(0, tc)], r_ref.at[slot, k], sem.at[slot]).wait()
    route = route_ref[...]
    g1 = route[:, 2:3]
    g2 = route[:, 3:4]
    d = x_ref.shape[-1]
    y = x_ref[...] + (g1 * r_ref[slot, 0].reshape(tc, d) + g2 * r_ref[slot, 1].reshape(tc, d))
    res = _rms(y, g_ref[...])

    @pl.when(i < n_first)
    def _():
        outp_ref[...] = res

    @pl.when(i >= n_first)
    def _():
        outs_ref[...] = res


def combine_norm(x, route, pos, expert_out, g, t_first, tc=256):
    t, d = x.shape
    tile = expert_out.shape[1:]
    n_first = t_first // tc
    grid_spec = pltpu.PrefetchScalarGridSpec(
        num_scalar_prefetch=1, grid=(t // tc,),
        in_specs=[pl.BlockSpec((tc, d), lambda i, pos: (i, 0)),
                  pl.BlockSpec((tc, LANES), lambda i, pos: (i, 0)),
                  pl.BlockSpec((1, d), lambda i, pos: (0, 0)),
                  pl.BlockSpec(memory_space=pl.ANY)],
        out_specs=[pl.BlockSpec((tc, d), lambda i, pos: (jnp.minimum(i, n_first - 1), 0)),
                   pl.BlockSpec((tc, d), lambda i, pos: (jnp.maximum(i - n_first, 0), 0))],
        scratch_shapes=[pltpu.VMEM((2, TOP_K, tc) + tile, F32), pltpu.SemaphoreType.DMA((2,))])
    return pl.pallas_call(
        functools.partial(_combine_norm_kernel, tc=tc, n_first=n_first), grid_spec=grid_spec,
        out_shape=[jax.ShapeDtypeStruct((t_first, d), F32), jax.ShapeDtypeStruct((t - t_first, d), F32)],
        compiler_params=_params("arbitrary"), name="combine_norm")(pos, x, route, g.reshape(1, d), expert_out)


def _dispatch_plan(e_slot, n_exp, mblk, ch, n_blocks, rows_g):
    n_slots = e_slot.shape[0]
    onehot = (e_slot[:, None] == jnp.arange(n_exp, dtype=jnp.int32)[None, :]).astype(jnp.int32)
    seg = 512
    oh = onehot.astype(F32).reshape(n_slots // seg, seg, n_exp)
    within = jnp.einsum("ts,bse->bte", jnp.tril(jnp.ones((seg, seg), F32)), oh, precision=lax.Precision.HIGHEST)
    totals = within[:, -1, :]
    csum = (within + (jnp.cumsum(totals, axis=0) - totals)[:, None, :]).reshape(n_slots, n_exp).astype(jnp.int32)
    rank = jnp.sum(csum * onehot, axis=1) - 1
    counts = csum[-1]
    nblk = (counts + mblk - 1) // mblk
    blk_end = jnp.cumsum(nblk)
    blk_start = blk_end - nblk
    dest = blk_start[e_slot] * mblk + rank
    n_used = blk_end[-1]
    step = jnp.arange(n_blocks, dtype=jnp.int32)
    blk_idx = jnp.minimum(step, n_used - 1)
    blk_exp = jnp.minimum(jnp.sum((blk_idx[:, None] >= blk_end[None, :]).astype(jnp.int32), axis=1), n_exp - 1)
    rows_in = jnp.clip(counts[blk_exp] - (blk_idx - blk_start[blk_exp]) * mblk, 0, mblk)
    blk_nch = jnp.where(step < n_used, (rows_in + ch - 1) // ch, 0).astype(jnp.int32)
    src = jnp.zeros((n_blocks * mblk,), jnp.int32).at[dest].set(jnp.arange(n_slots, dtype=jnp.int32) // TOP_K)
    first_row = jnp.arange(mblk // rows_g, dtype=jnp.int32)[None, :] * rows_g
    gather_valid = (first_row < blk_nch[:, None] * ch).astype(jnp.int32).reshape(-1)
    return dest.astype(jnp.int32), src, blk_idx.astype(jnp.int32), blk_exp.astype(jnp.int32), blk_nch, gather_valid


def kernel(x_prompt, x_sample, state_conv, norm_mix, norm_ffn, norm_final, conv_w_in, conv_b_in, conv_w_dw, conv_b_dw, conv_ln_g, conv_ln_b, conv_w_out, conv_b_out, gmlp_w_in, gmlp_b_in, gmlp_ln_g, gmlp_ln_b, gmlp_w_s, gmlp_b_s, gmlp_w_out, gmlp_b_out, ffn_w_gate, ffn_w_up, ffn_w_down, moe_router, moe_w_gate, moe_w_up, moe_w_down):
    nb_p, len_p, d = x_prompt.shape
    nb_s, len_s, _ = x_sample.shape
    tp, ts = nb_p * len_p, nb_s * len_s
    t = tp + ts
    hist_rows = CONV_W - 1

    x, h = concat_rmsnorm(x_prompt.reshape(tp, d), x_sample.reshape(ts, d), norm_mix[0])
    glu_t = matmul_glu(h, conv_w_in[0], conv_b_in[0])
    y = conv_ln_silu(glu_t, state_conv[0], conv_w_dw[0], conv_b_dw[0], conv_ln_g[0], conv_ln_b[0],
                     nb_p, len_p, nb_s, len_s)
    x, h = matmul_residual(y, conv_w_out[0], conv_b_out[0], x, norm_ffn[0])
    tails = jnp.stack([glu_t[:, (b + 1) * len_p - hist_rows:(b + 1) * len_p] for b in range(nb_p)])
    new_conv_prompt = tails.transpose(0, 2, 1, 3).reshape(nb_p, hist_rows, d)[None]
    glu_s = glu_t[:, tp:].reshape(LANE_TILES, nb_s, len_s, LANES).transpose(1, 2, 0, 3).reshape(nb_s, len_s, d)
    new_conv_sample = jnp.concatenate([state_conv[0], glu_s], axis=1)[:, len_s:][None]

    mblk0, ch0 = 1152, 576
    nb0 = t // mblk0
    ffn = grouped_swiglu(h, ffn_w_gate, ffn_w_up, ffn_w_down,
                         jnp.arange(nb0, dtype=jnp.int32), jnp.zeros((nb0,), jnp.int32),
                         jnp.full((nb0,), mblk0 // ch0, jnp.int32), mblk0, ch0, FFN_TF, FFN_TK, FFN_TN, False)

    x, h = add_rmsnorm(x, ffn, norm_mix[1])
    z = matmul_gelu(h, gmlp_w_in[0], gmlp_b_in[0])
    tril = jnp.tril(jnp.ones((CHUNK, CHUNK), dtype=bool))
    w_prompt = jnp.where(tril[None], gmlp_w_s[0], 0)
    seqs = CHUNK // len_s
    w_small = jnp.where(tril[None, :len_s, :len_s], gmlp_w_s[0][:, :len_s, :len_s], 0)
    w_sample = jnp.einsum("ab,gts->gatbs", jnp.eye(seqs, dtype=F32), w_small).reshape(N_SG, CHUNK, CHUNK)
    w_mix = jnp.stack([w_prompt, w_sample]).astype(BF16)
    bias_rows = jnp.stack([gmlp_b_s[0].T, jnp.tile(gmlp_b_s[0].T[:len_s], (seqs, 1))])
    bias_mix = jnp.repeat(bias_rows, SG_DIM, axis=2)
    y, v = gmlp_mix(z, w_mix, bias_mix, gmlp_ln_g[0], gmlp_ln_b[0], tp // CHUNK)
    x = matmul_residual(y, gmlp_w_out[0], gmlp_b_out[0], x)
    new_chunk_v_sample = v.reshape(nb_s, len_s, d)[None]

    h_t, route = norm_route(x, norm_ffn[1], moe_router[0])
    e_slot = route[:, :TOP_K].astype(jnp.int32).reshape(-1)
    mblk1, ch1, rows_g = 1280, 320, 640
    n_blocks = (t * TOP_K) // mblk1 + N_EXP
    dest, src, blk_idx, blk_exp, blk_nch, gather_valid = _dispatch_plan(e_slot, N_EXP, mblk1, ch1, n_blocks, rows_g)
    xs = gather_rows(h_t, src, gather_valid, rows_g)
    eo = grouped_swiglu(xs, moe_w_gate[0], moe_w_up[0], moe_w_down[0], blk_idx, blk_exp, blk_nch, mblk1, ch1,
                        FFN_TF, FFN_TK, FFN_TN, True)
    out_p, out_s = combine_norm(x, route, dest, eo, norm_final, tp)
    return (out_p.reshape(nb_p, len_p, d), out_s.reshape(nb_s, len_s, d), new_conv_prompt, new_conv_sample,
            new_chunk_v_sample)
```

```python
import functools

import jax
import jax.numpy as jnp
from jax import lax
from jax.experimental import pallas as pl
from jax.experimental.pallas import tpu as pltpu

F32 = jnp.float32
BF16 = jnp.bfloat16

D_MODEL = 2048
CONV_W = 31
CHUNK = 128
N_SG = 8
SG_DIM = D_MODEL // N_SG
N_EXP = 8
TOP_K = 2
EPS_RMS = 1e-6
EPS_LN = 1e-5

V7X_VMEM_BYTES = 64 * 1024 * 1024
VMEM_LIMIT = V7X_VMEM_BYTES - 8 * 1024 * 1024
LANES = 128
LANE_TILES = D_MODEL // LANES
HALO = 32
FFN_TF, FFN_TK, FFN_TN = 256, 1024, 1024
ROW_DMA_UNROLL = 8


def _params(*sem):
    return pltpu.CompilerParams(dimension_semantics=sem, vmem_limit_bytes=VMEM_LIMIT)


def _rms(x, g):
    return x * lax.rsqrt(jnp.mean(x * x, axis=-1, keepdims=True) + EPS_RMS) * g


def _layernorm(x, g, b):
    mu = jnp.mean(x, axis=-1, keepdims=True)
    xc = x - mu
    return xc * lax.rsqrt(jnp.mean(xc * xc, axis=-1, keepdims=True) + EPS_LN) * g + b


def _concat_norm_kernel(xa_ref, xb_ref, g_ref, xo_ref, h_ref, *, n_first):
    def emit(x):
        xo_ref[...] = x
        h_ref[...] = _rms(x, g_ref[...]).astype(BF16)

    @pl.when(pl.program_id(0) < n_first)
    def _():
        emit(xa_ref[...])

    @pl.when(pl.program_id(0) >= n_first)
    def _():
        emit(xb_ref[...])


def _add_norm_kernel(x_ref, d_ref, g_ref, xo_ref, h_ref):
    x = x_ref[...] + d_ref[...]
    xo_ref[...] = x
    h_ref[...] = _rms(x, g_ref[...]).astype(BF16)


def concat_rmsnorm(xa, xb, g, tr=512):
    ta, d = xa.shape
    t = ta + xb.shape[0]
    n_first = ta // tr
    row = pl.BlockSpec((tr, d), lambda i: (i, 0))
    return pl.pallas_call(
        functools.partial(_concat_norm_kernel, n_first=n_first), grid=(t // tr,),
        in_specs=[pl.BlockSpec((tr, d), lambda i: (jnp.minimum(i, n_first - 1), 0)),
                  pl.BlockSpec((tr, d), lambda i: (jnp.maximum(i - n_first, 0), 0)),
                  pl.BlockSpec((1, d), lambda i: (0, 0))],
        out_specs=[row, row],
        out_shape=[jax.ShapeDtypeStruct((t, d), F32), jax.ShapeDtypeStruct((t, d), BF16)],
        compiler_params=_params("arbitrary"), name="concat_rmsnorm")(xa, xb, g.reshape(1, d))


def add_rmsnorm(x, delta, g, tr=512):
    t, d = x.shape
    row = pl.BlockSpec((tr, d), lambda i: (i, 0))
    return pl.pallas_call(
        _add_norm_kernel, grid=(t // tr,), in_specs=[row, row, pl.BlockSpec((1, d), lambda i: (0, 0))],
        out_specs=[row, row],
        out_shape=[jax.ShapeDtypeStruct((t, d), F32), jax.ShapeDtypeStruct((t, d), BF16)],
        compiler_params=_params("parallel"), name="add_rmsnorm")(x, delta, g.reshape(1, d))


def _gelu_exact(z):
    return 0.5 * z * (1.0 + lax.erf(z * (0.5 ** 0.5)))


def _cast_on_first_row_tile(w_ref, ws_ref):
    @pl.when(pl.program_id(1) == 0)
    def _():
        ws_ref[...] = w_ref[...].astype(BF16)


def _mm_glu_kernel(a_ref, wa_ref, wg_ref, ba_ref, bg_ref, o_ref, was_ref, wgs_ref):
    _cast_on_first_row_tile(wa_ref, was_ref)
    _cast_on_first_row_tile(wg_ref, wgs_ref)
    act = a_ref[...]
    a = jnp.dot(act, was_ref[...], preferred_element_type=F32) + ba_ref[...]
    gate = jnp.dot(act, wgs_ref[...], preferred_element_type=F32) + bg_ref[...]
    glu = a * jax.nn.sigmoid(gate)
    for j in range(o_ref.shape[0]):
        o_ref[j] = glu[:, j * LANES:(j + 1) * LANES]


def _mm_gelu_kernel(a_ref, w_ref, b_ref, o_ref, ws_ref):
    _cast_on_first_row_tile(w_ref, ws_ref)
    z = jnp.dot(a_ref[...], ws_ref[...], preferred_element_type=F32) + b_ref[...]
    o_ref[...] = _gelu_exact(z)


def _top2_route(h, rt_ref):
    r = rt_ref[...]
    h_hi = h.astype(BF16)
    h_lo = (h - h_hi.astype(F32)).astype(BF16)
    r_hi = r.astype(BF16)
    r_lo = (r - r_hi.astype(F32)).astype(BF16)
    nt = (((1,), (1,)), ((), ()))
    logits = (lax.dot_general(r_hi, h_hi, nt, preferred_element_type=F32)
              + lax.dot_general(r_lo, h_hi, nt, preferred_element_type=F32)
              + lax.dot_general(r_hi, h_lo, nt, preferred_element_type=F32))
    expert = lax.broadcasted_iota(jnp.int32, logits.shape, 0).astype(F32)
    neg = jnp.float32(-jnp.inf)
    m1 = jnp.max(logits, axis=0, keepdims=True)
    i1 = jnp.min(jnp.where(logits == m1, expert, float(N_EXP)), axis=0, keepdims=True)
    rest = jnp.where(expert == i1, neg, logits)
    m2 = jnp.max(rest, axis=0, keepdims=True)
    i2 = jnp.min(jnp.where(rest == m2, expert, float(N_EXP)), axis=0, keepdims=True)
    e2 = jnp.exp(m2 - m1)
    g1 = 1.0 / (1.0 + e2)
    g2 = e2 / (1.0 + e2)
    route = jnp.where(expert == 0, i1, 0.0)
    route = jnp.where(expert == 1, i2, route)
    route = jnp.where(expert == 2, g1, route)
    return jnp.where(expert == 3, g2, route)


def _mm_res_kernel(a_ref, w_ref, b_ref, r_ref, *rest, mode):
    if mode == "route":
        g_ref, rt_ref, x_ref, h_ref, route_ref, ws_ref = rest
    elif mode == "norm":
        g_ref, x_ref, h_ref, ws_ref = rest
    else:
        x_ref, ws_ref = rest

    @pl.when(pl.program_id(0) == 0)
    def _():
        ws_ref[...] = w_ref[...].astype(BF16)

    x = r_ref[...] + (jnp.dot(a_ref[...], ws_ref[...], preferred_element_type=F32) + b_ref[...])
    x_ref[...] = x
    if mode != "plain":
        h = _rms(x, g_ref[...])
        h_ref[...] = h.astype(BF16).reshape(h_ref.shape)
    if mode == "route":
        route_ref[...] = _top2_route(h, rt_ref)


def matmul_glu(act, w, b, tm=1024, tn=512):
    t, k = act.shape
    n = w.shape[1] // 2
    nb = n // tn
    b = b.reshape(1, 2 * n)
    return pl.pallas_call(
        _mm_glu_kernel, grid=(nb, t // tm),
        in_specs=[pl.BlockSpec((tm, k), lambda j, i: (i, 0)),
                  pl.BlockSpec((k, tn), lambda j, i: (0, j)),
                  pl.BlockSpec((k, tn), lambda j, i: (0, j + nb)),
                  pl.BlockSpec((1, tn), lambda j, i: (0, j)),
                  pl.BlockSpec((1, tn), lambda j, i: (0, j + nb))],
        out_specs=pl.BlockSpec((tn // LANES, tm, LANES), lambda j, i: (j, i, 0)),
        out_shape=jax.ShapeDtypeStruct((n // LANES, t, LANES), F32),
        scratch_shapes=[pltpu.VMEM((k, tn), BF16), pltpu.VMEM((k, tn), BF16)],
        compiler_params=_params("parallel", "arbitrary"), name="matmul_glu")(act, w, w, b, b)


def matmul_gelu(act, w, b, tm=1024, tn=1024):
    t, k = act.shape
    n = w.shape[1]
    return pl.pallas_call(
        _mm_gelu_kernel, grid=(n // tn, t // tm),
        in_specs=[pl.BlockSpec((tm, k), lambda j, i: (i, 0)),
                  pl.BlockSpec((k, tn), lambda j, i: (0, j)),
                  pl.BlockSpec((1, tn), lambda j, i: (0, j))],
        out_specs=pl.BlockSpec((tm, tn), lambda j, i: (i, j)),
        out_shape=jax.ShapeDtypeStruct((t, n), F32),
        scratch_shapes=[pltpu.VMEM((k, tn), BF16)],
        compiler_params=_params("parallel", "arbitrary"), name="matmul_gelu")(act, w, b.reshape(1, n))


def matmul_residual(act, w, b, res, g=None, router=None, tm=512):
    t, k = act.shape
    n = w.shape[1]
    mode = "plain" if g is None else ("norm" if router is None else "route")
    row = lambda width: pl.BlockSpec((tm, width), lambda i: (i, 0))
    fixed = lambda shape: pl.BlockSpec(shape, lambda i: (0, 0))
    in_specs = [row(k), pl.BlockSpec((k, n), lambda i: (0, 0), pipeline_mode=pl.Buffered(1)), fixed((1, n)), row(n)]
    args = [act, w, b.reshape(1, n), res]
    out_specs, out_shape = row(n), jax.ShapeDtypeStruct((t, n), F32)
    if mode != "plain":
        in_specs.append(fixed((1, n)))
        args.append(g.reshape(1, n))
        out_specs, out_shape = [out_specs, row(n)], [out_shape, jax.ShapeDtypeStruct((t, n), BF16)]
    if mode == "route":
        in_specs.append(fixed((N_EXP, n)))
        args.append(router.T)
        out_specs = [out_specs[0], pl.BlockSpec((tm, n // LANES, LANES), lambda i: (i, 0, 0)),
                     pl.BlockSpec((N_EXP, tm), lambda i: (0, i))]
        out_shape = [out_shape[0], jax.ShapeDtypeStruct((t, n // LANES, LANES), BF16),
                     jax.ShapeDtypeStruct((N_EXP, t), F32)]
    return pl.pallas_call(
        functools.partial(_mm_res_kernel, mode=mode), grid=(t // tm,),
        in_specs=in_specs, out_specs=out_specs, out_shape=out_shape,
        scratch_shapes=[pltpu.VMEM((k, n), BF16)],
        compiler_params=_params("arbitrary"), name="matmul_residual_" + mode)(*args)


def _conv_taps(xp_ref, y_ref, wdw_ref, bdw_ref, l, x0, y0, tt, rows):
    first = HALO - (CONV_W - 1)
    for r0 in range(0, tt, rows):
        acc = jnp.broadcast_to(bdw_ref[l], (rows, LANES))
        for k in range(CONV_W):
            lo = x0 + r0 + first + k
            acc = acc + xp_ref[l, lo:lo + rows, :] * wdw_ref[l, k:k + 1, :]
        y_ref[l, y0 + r0:y0 + r0 + rows, :] = acc


def _ln_silu_store(y_ref, lng_ref, lnb_ref, o_ref):
    n = y_ref.shape[1]
    d = LANE_TILES * LANES
    s1 = y_ref[0]
    for l in range(1, LANE_TILES):
        s1 = s1 + y_ref[l]
    mu = jnp.sum(s1, axis=-1, keepdims=True) * (1.0 / d)
    s2 = jnp.zeros((n, LANES), F32)
    for l in range(LANE_TILES):
        c = y_ref[l] - mu
        s2 = s2 + c * c
    inv = lax.rsqrt(jnp.sum(s2, axis=-1, keepdims=True) * (1.0 / d) + EPS_LN)
    for l in range(LANE_TILES):
        y = (y_ref[l] - mu) * inv * lng_ref[l] + lnb_ref[l]
        o_ref[:, l * LANES:(l + 1) * LANES] = (y * jax.nn.sigmoid(y)).astype(BF16)


def _conv_prompt_kernel(halo_ref, main_ref, wdw_ref, bdw_ref, lng_ref, lnb_ref, o_ref, xp_ref, y_ref, *, tt):
    xp_ref[:, 0:HALO, :] = jnp.where(pl.program_id(1) > 0, halo_ref[...], 0.0)
    xp_ref[:, HALO:HALO + tt, :] = main_ref[...]

    def lane_tile(l, carry):
        _conv_taps(xp_ref, y_ref, wdw_ref, bdw_ref, l, 0, 0, tt, 32)
        return carry

    lax.fori_loop(0, LANE_TILES, lane_tile, 0)
    _ln_silu_store(y_ref, lng_ref, lnb_ref, o_ref)


def _conv_sample_kernel(hist_ref, main_ref, wdw_ref, bdw_ref, lng_ref, lnb_ref, o_ref, xp_ref, y_ref, *, ts, sb):
    span = HALO + ts
    pad = HALO - hist_ref.shape[1]
    for s in range(sb):
        xp_ref[:, s * span:s * span + pad, :] = jnp.zeros((LANE_TILES, pad, LANES), F32)
        for l in range(LANE_TILES):
            xp_ref[l, s * span + pad:s * span + HALO, :] = hist_ref[s, :, l * LANES:(l + 1) * LANES]
        xp_ref[:, s * span + HALO:(s + 1) * span, :] = main_ref[:, s * ts:(s + 1) * ts, :]

    def lane_tile(l, carry):
        for s in range(sb):
            _conv_taps(xp_ref, y_ref, wdw_ref, bdw_ref, l, s * span, s * ts, ts, ts)
        return carry

    lax.fori_loop(0, LANE_TILES, lane_tile, 0)
    _ln_silu_store(y_ref, lng_ref, lnb_ref, o_ref)


def _lane_tiled(v):
    return v.reshape(v.shape[0], LANE_TILES, LANES).transpose(1, 0, 2)


def conv_ln_silu(glu_t, hist, w_dw, b_dw, ln_g, ln_b, n_prompt_seq, prompt_len, n_sample_seq, sample_len, tt=256):
    lt, t, _ = glu_t.shape
    d = lt * LANES
    tp = n_prompt_seq * prompt_len
    z3 = lambda *_: (0, 0, 0)
    wspecs = [pl.BlockSpec((lt, CONV_W, LANES), z3), pl.BlockSpec((lt, 1, LANES), z3),
              pl.BlockSpec((lt, 1, LANES), z3), pl.BlockSpec((lt, 1, LANES), z3)]
    wargs = (_lane_tiled(w_dw), _lane_tiled(b_dw.reshape(1, d)), _lane_tiled(ln_g.reshape(1, d)),
             _lane_tiled(ln_b.reshape(1, d)))
    tiles = prompt_len // tt
    per = tt // HALO
    y_p = pl.pallas_call(
        functools.partial(_conv_prompt_kernel, tt=tt), grid=(n_prompt_seq, tiles),
        in_specs=[pl.BlockSpec((lt, HALO, LANES), lambda b, i: (0, jnp.maximum((b * tiles + i) * per - 1, 0), 0)),
                  pl.BlockSpec((lt, tt, LANES), lambda b, i: (0, b * tiles + i, 0))] + wspecs,
        out_specs=pl.BlockSpec((tt, d), lambda b, i: (b * tiles + i, 0)),
        out_shape=jax.ShapeDtypeStruct((tp, d), BF16),
        scratch_shapes=[pltpu.VMEM((lt, HALO + tt, LANES), F32), pltpu.VMEM((lt, tt, LANES), F32)],
        compiler_params=_params("parallel", "parallel"), name="conv_prompt")(glu_t, glu_t, *wargs)
    ts = sample_len
    sb = 16
    base = tp // (sb * ts)
    y_s = pl.pallas_call(
        functools.partial(_conv_sample_kernel, ts=ts, sb=sb), grid=(n_sample_seq // sb,),
        in_specs=[pl.BlockSpec((sb,) + hist.shape[1:], lambda b: (b, 0, 0)),
                  pl.BlockSpec((lt, sb * ts, LANES), lambda b: (0, base + b, 0))] + wspecs,
        out_specs=pl.BlockSpec((sb * ts, d), lambda b: (b, 0)),
        out_shape=jax.ShapeDtypeStruct((n_sample_seq * ts, d), BF16),
        scratch_shapes=[pltpu.VMEM((lt, sb * (HALO + ts), LANES), F32), pltpu.VMEM((lt, sb * ts, LANES), F32)],
        compiler_params=_params("parallel"), name="conv_sample")(hist, glu_t, *wargs)
    return jnp.concatenate([y_p, y_s], axis=0)


def _gmlp_mix_kernel(u_ref, v_ref, w_ref, bias_ref, lng_ref, lnb_ref, y_ref, vn_ref, *, n_prompt_chunks):
    vn = _layernorm(v_ref[...], lng_ref[...], lnb_ref[...])

    @pl.when(pl.program_id(0) >= n_prompt_chunks)
    def _():
        vn_ref[...] = vn

    vb = vn.astype(BF16)
    for g in range(N_SG):
        cols = slice(g * SG_DIM, (g + 1) * SG_DIM)
        mixed = jnp.dot(w_ref[0, g], vb[:, cols], preferred_element_type=F32) + bias_ref[0, :, cols]
        y_ref[:, cols] = (u_ref[:, cols] * mixed).astype(BF16)


def gmlp_mix(z, w_mix, bias_mix, ln_g, ln_b, n_prompt_chunks):
    t = z.shape[0]
    d = z.shape[1] // 2
    which = lambda c: (c >= n_prompt_chunks).astype(jnp.int32)
    return pl.pallas_call(
        functools.partial(_gmlp_mix_kernel, n_prompt_chunks=n_prompt_chunks), grid=(t // CHUNK,),
        in_specs=[pl.BlockSpec((CHUNK, d), lambda c: (c, 0)),
                  pl.BlockSpec((CHUNK, d), lambda c: (c, 1)),
                  pl.BlockSpec((1, N_SG, CHUNK, CHUNK), lambda c: (which(c), 0, 0, 0)),
                  pl.BlockSpec((1, CHUNK, d), lambda c: (which(c), 0, 0)),
                  pl.BlockSpec((1, d), lambda c: (0, 0)),
                  pl.BlockSpec((1, d), lambda c: (0, 0))],
        out_specs=[pl.BlockSpec((CHUNK, d), lambda c: (c, 0)),
                   pl.BlockSpec((CHUNK, d), lambda c: (jnp.maximum(c - n_prompt_chunks, 0), 0))],
        out_shape=[jax.ShapeDtypeStruct((t, d), BF16), jax.ShapeDtypeStruct((t - n_prompt_chunks * CHUNK, d), F32)],
        compiler_params=_params("arbitrary"), name="gmlp_mix")(
            z, z, w_mix, bias_mix, ln_g.reshape(1, d), ln_b.reshape(1, d))


def _ffn_kernel(blk_ref, exp_ref, nch_ref, x_ref, wg_ref, wu_ref, wd_ref, o_ref, a_ref, acc_ref, *,
                n_chunks, ch, nf, nk, tiled):
    b = pl.program_id(0)
    s = pl.program_id(1)
    nch = nch_ref[b]
    tf = wg_ref.shape[-1]
    tk = wd_ref.shape[1]
    kk = lax.rem(jnp.maximum(s - nf, 0), nk)

    def store_out(rows, val):
        o_ref[rows] = val.reshape((rows.size,) + o_ref.shape[1:]) if tiled else val

    def up_gate(rows):
        xr = x_ref[rows, :]
        g = jnp.dot(xr, wg_ref[0].astype(BF16), preferred_element_type=F32)
        u = jnp.dot(xr, wu_ref[0].astype(BF16), preferred_element_type=F32)
        a_ref[rows, pl.ds(pl.multiple_of(s * tf, tf), tf)] = (g * jax.nn.sigmoid(g) * u).astype(BF16)

    def down(rows):
        slab = pl.ds(pl.multiple_of(kk * tk, tk), tk)
        part = jnp.dot(a_ref[rows, slab], wd_ref[0].astype(BF16), preferred_element_type=F32)
        acc_ref[rows, :] = part + jnp.where(kk > 0, acc_ref[rows, :], 0.0)

        @pl.when(kk == nk - 1)
        def _():
            store_out(rows, acc_ref[rows, :])

    @pl.when(jnp.logical_and(b == 0, s == 0))
    def _():
        acc_ref[...] = jnp.zeros(acc_ref.shape, F32)

    def run(step_fn):
        @pl.when(nch == n_chunks)
        def _():
            step_fn(pl.ds(0, n_chunks * ch))

        @pl.when(jnp.logical_and(nch > 0, nch < n_chunks))
        def _():
            for c in range(n_chunks - 1):
                @pl.when(c < nch)
                def _():
                    step_fn(pl.ds(c * ch, ch))

    @pl.when(s < nf)
    def _():
        run(up_gate)

    @pl.when(s >= nf)
    def _():
        run(down)

    @pl.when(jnp.logical_and(s >= nf, kk == nk - 1))
    def _():
        for c in range(n_chunks):
            @pl.when(c >= nch)
            def _():
                store_out(pl.ds(c * ch, ch), jnp.zeros((ch, acc_ref.shape[-1]), F32))


def grouped_swiglu(x, wg, wu, wd, blk_idx, blk_exp, blk_nch, mblk, ch, tf, tk, tn, tiled):
    p, d = x.shape
    dff = wg.shape[-1]
    nb = blk_idx.shape[0]
    nf, nk, nn = dff // tf, dff // tk, d // tn
    up_tile = lambda b, s, nch: jnp.where(nch[b] > 0, jnp.minimum(s, nf - 1), nf - 1)
    down_step = lambda b, s, nch: jnp.where(nch[b] > 0, jnp.maximum(s - nf, 0), nn * nk - 1)
    out_tile = lambda s: jnp.maximum(s - nf, 0) // nk
    once = dict(pipeline_mode=pl.Buffered(1))
    x_spec = pl.BlockSpec((mblk, d), lambda b, s, blk, ex, nch: (blk[b], 0), **once)
    if tiled:
        out_spec = pl.BlockSpec((mblk, tn // LANES, LANES), lambda b, s, blk, ex, nch: (b, out_tile(s), 0), **once)
        out_shape = jax.ShapeDtypeStruct((p, d // LANES, LANES), F32)
    else:
        out_spec = pl.BlockSpec((mblk, tn), lambda b, s, blk, ex, nch: (b, out_tile(s)), **once)
        out_shape = jax.ShapeDtypeStruct((p, d), F32)
    grid_spec = pltpu.PrefetchScalarGridSpec(
        num_scalar_prefetch=3, grid=(nb, nf + nn * nk),
        in_specs=[x_spec,
                  pl.BlockSpec((1, d, tf), lambda b, s, blk, ex, nch: (ex[b], 0, up_tile(b, s, nch))),
                  pl.BlockSpec((1, d, tf), lambda b, s, blk, ex, nch: (ex[b], 0, up_tile(b, s, nch))),
                  pl.BlockSpec((1, tk, tn), lambda b, s, blk, ex, nch: (
                      ex[b], lax.rem(down_step(b, s, nch), nk), down_step(b, s, nch) // nk))],
        out_specs=out_spec,
        scratch_shapes=[pltpu.VMEM((mblk, dff), BF16), pltpu.VMEM((mblk, tn), F32)])
    return pl.pallas_call(
        functools.partial(_ffn_kernel, n_chunks=mblk // ch, ch=ch, nf=nf, nk=nk, tiled=tiled), grid_spec=grid_spec,
        out_shape=out_shape, compiler_params=_params("arbitrary", "arbitrary"),
        name="grouped_swiglu_tiled" if tiled else "grouped_swiglu")(blk_idx, blk_exp, blk_nch, x, wg, wu, wd)


def _gather_rows_kernel(src_ref, valid_ref, h_ref, o_ref, buf_ref, sem, *, rows):
    c = pl.program_id(0)
    last = pl.num_programs(0) - 1
    slot = lax.rem(c, 2)

    def request(j, slot_j):
        def body(i, carry):
            for u in range(ROW_DMA_UNROLL):
                r = i * ROW_DMA_UNROLL + u
                pltpu.make_async_copy(h_ref.at[pl.ds(src_ref[j * rows + r], 1)],
                                      buf_ref.at[slot_j, pl.ds(r, 1)], sem.at[slot_j]).start(priority=u % 2)
            return carry

        lax.fori_loop(0, rows // ROW_DMA_UNROLL, body, 0)

    @pl.when(jnp.logical_and(c == 0, valid_ref[0] > 0))
    def _():
        request(0, 0)

    nxt = jnp.minimum(c + 1, last)

    @pl.when(jnp.logical_and(c < last, valid_ref[nxt] > 0))
    def _():
        request(nxt, 1 - slot)

    @pl.when(valid_ref[c] > 0)
    def _():
        pltpu.make_async_copy(h_ref.at[pl.ds(0, rows)], buf_ref.at[slot], sem.at[slot]).wait()
        o_ref[...] = buf_ref[slot].reshape(o_ref.shape)

    @pl.when(valid_ref[c] == 0)
    def _():
        o_ref[...] = jnp.zeros(o_ref.shape, o_ref.dtype)


def gather_rows(h, src, valid, rows):
    p = src.shape[0]
    d = h.shape[1] * h.shape[2]
    grid_spec = pltpu.PrefetchScalarGridSpec(
        num_scalar_prefetch=2, grid=(p // rows,),
        in_specs=[pl.BlockSpec(memory_space=pl.ANY)],
        out_specs=pl.BlockSpec((rows, d), lambda c, src, valid: (c, 0)),
        scratch_shapes=[pltpu.VMEM((2, rows) + h.shape[1:], h.dtype), pltpu.SemaphoreType.DMA((2,))])
    return pl.pallas_call(
        functools.partial(_gather_rows_kernel, rows=rows), grid_spec=grid_spec,
        out_shape=jax.ShapeDtypeStruct((p, d), h.dtype),
        compiler_params=_params("arbitrary"), name="gather_rows")(src, valid, h)


def _combine_norm_kernel(pos_ref, x_ref, route_ref, g_ref, o_hbm, outp_ref, outs_ref, r_ref, sem, *, tc, n_first):
    i = pl.program_id(0)
    last = pl.num_programs(0) - 1
    slot = lax.rem(i, 2)

    def request(j, slot_j):
        def body(q, carry):
            for u in range(ROW_DMA_UNROLL // TOP_K):
                r = q * (ROW_DMA_UNROLL // TOP_K) + u
                for k in range(TOP_K):
                    pltpu.make_async_copy(o_hbm.at[pl.ds(pos_ref[TOP_K * (j * tc + r) + k], 1)],
                                          r_ref.at[slot_j, k, pl.ds(r, 1)], sem.at[slot_j]).start(priority=k % 2)
            return carry

        lax.fori_loop(0, tc * TOP_K // ROW_DMA_UNROLL, body, 0)

    @pl.when(i == 0)
    def _():
        request(0, 0)

    @pl.when(i < last)
    def _():
        request(i + 1, 1 - slot)

    for k in range(TOP_K):
        pltpu.make_async_copy(o_hbm.at[pl.ds(0, tc)], r_ref.at[slot, k], sem.at[slot]).wait()
    route = route_ref[...]
    g1 = route[:, 2:3]
    g2 = route[:, 3:4]
    d = x_ref.shape[-1]
    y = x_ref[...] + (g1 * r_ref[slot, 0].reshape(tc, d) + g2 * r_ref[slot, 1].reshape(tc, d))
    res = _rms(y, g_ref[...])

    @pl.when(i < n_first)
    def _():
        outp_ref[...] = res

    @pl.when(i >= n_first)
    def _():
        outs_ref[...] = res


def combine_norm(x, route, pos, expert_out, g, t_first, tc=256):
    t, d = x.shape
    tile = expert_out.shape[1:]
    n_first = t_first // tc
    grid_spec = pltpu.PrefetchScalarGridSpec(
        num_scalar_prefetch=1, grid=(t // tc,),
        in_specs=[pl.BlockSpec((tc, d), lambda i, pos: (i, 0)),
                  pl.BlockSpec((tc, LANES), lambda i, pos: (i, 0)),
                  pl.BlockSpec((1, d), lambda i, pos: (0, 0)),
                  pl.BlockSpec(memory_space=pl.ANY)],
        out_specs=[pl.BlockSpec((tc, d), lambda i, pos: (jnp.minimum(i, n_first - 1), 0)),
                   pl.BlockSpec((tc, d), lambda i, pos: (jnp.maximum(i - n_first, 0), 0))],
        scratch_shapes=[pltpu.VMEM((2, TOP_K, tc) + tile, F32), pltpu.SemaphoreType.DMA((2,))])
    return pl.pallas_call(
        functools.partial(_combine_norm_kernel, tc=tc, n_first=n_first), grid_spec=grid_spec,
        out_shape=[jax.ShapeDtypeStruct((t_first, d), F32), jax.ShapeDtypeStruct((t - t_first, d), F32)],
        compiler_params=_params("arbitrary"), name="combine_norm")(pos, x, route, g.reshape(1, d), expert_out)


def _dispatch_plan(e_slot, n_exp, mblk, ch, n_blocks, rows_g):
    n_slots = e_slot.shape[0]
    onehot = (e_slot[:, None] == jnp.arange(n_exp, dtype=jnp.int32)[None, :]).astype(jnp.int32)
    seg = 512
    oh = onehot.astype(F32).reshape(n_slots // seg, seg, n_exp)
    within = jnp.einsum("ts,bse->bte", jnp.tril(jnp.ones((seg, seg), F32)), oh, precision=lax.Precision.HIGHEST)
    totals = within[:, -1, :]
    csum = (within + (jnp.cumsum(totals, axis=0) - totals)[:, None, :]).reshape(n_slots, n_exp).astype(jnp.int32)
    rank = jnp.sum(csum * onehot, axis=1) - 1
    counts = csum[-1]
    nblk = (counts + mblk - 1) // mblk
    blk_end = jnp.cumsum(nblk)
    blk_start = blk_end - nblk
    dest = blk_start[e_slot] * mblk + rank
    n_used = blk_end[-1]
    step = jnp.arange(n_blocks, dtype=jnp.int32)
    blk_idx = jnp.minimum(step, n_used - 1)
    blk_exp = jnp.minimum(jnp.sum((blk_idx[:, None] >= blk_end[None, :]).astype(jnp.int32), axis=1), n_exp - 1)
    rows_in = jnp.clip(counts[blk_exp] - (blk_idx - blk_start[blk_exp]) * mblk, 0, mblk)
    blk_nch = jnp.where(step < n_used, (rows_in + ch - 1) // ch, 0).astype(jnp.int32)
    src = jnp.zeros((n_blocks * mblk,), jnp.int32).at[dest].set(jnp.arange(n_slots, dtype=jnp.int32) // TOP_K)
    first_row = jnp.arange(mblk // rows_g, dtype=jnp.int32)[None, :] * rows_g
    gather_valid = (first_row < blk_nch[:, None] * ch).astype(jnp.int32).reshape(-1)
    return dest.astype(jnp.int32), src, blk_idx.astype(jnp.int32), blk_exp.astype(jnp.int32), blk_nch, gather_valid


def kernel(x_prompt, x_sample, state_conv, norm_mix, norm_ffn, norm_final, conv_w_in, conv_b_in, conv_w_dw, conv_b_dw, conv_ln_g, conv_ln_b, conv_w_out, conv_b_out, gmlp_w_in, gmlp_b_in, gmlp_ln_g, gmlp_ln_b, gmlp_w_s, gmlp_b_s, gmlp_w_out, gmlp_b_out, ffn_w_gate, ffn_w_up, ffn_w_down, moe_router, moe_w_gate, moe_w_up, moe_w_down):
    nb_p, len_p, d = x_prompt.shape
    nb_s, len_s, _ = x_sample.shape
    tp, ts = nb_p * len_p, nb_s * len_s
    t = tp + ts
    hist_rows = CONV_W - 1

    x, h = concat_rmsnorm(x_prompt.reshape(tp, d), x_sample.reshape(ts, d), norm_mix[0])
    glu_t = matmul_glu(h, conv_w_in[0], conv_b_in[0])
    y = conv_ln_silu(glu_t, state_conv[0], conv_w_dw[0], conv_b_dw[0], conv_ln_g[0], conv_ln_b[0],
                     nb_p, len_p, nb_s, len_s)
    x, h = matmul_residual(y, conv_w_out[0], conv_b_out[0], x, norm_ffn[0])
    tails = jnp.stack([glu_t[:, (b + 1) * len_p - hist_rows:(b + 1) * len_p] for b in range(nb_p)])
    new_conv_prompt = tails.transpose(0, 2, 1, 3).reshape(nb_p, hist_rows, d)[None]
    glu_s = glu_t[:, tp:].reshape(LANE_TILES, nb_s, len_s, LANES).transpose(1, 2, 0, 3).reshape(nb_s, len_s, d)
    new_conv_sample = jnp.concatenate([state_conv[0], glu_s], axis=1)[:, len_s:][None]

    mblk0, ch0 = 1152, 576
    nb0 = t // mblk0
    ffn = grouped_swiglu(h, ffn_w_gate, ffn_w_up, ffn_w_down,
                         jnp.arange(nb0, dtype=jnp.int32), jnp.zeros((nb0,), jnp.int32),
                         jnp.full((nb0,), mblk0 // ch0, jnp.int32), mblk0, ch0, FFN_TF, FFN_TK, FFN_TN, False)

    x, h = add_rmsnorm(x, ffn, norm_mix[1])
    z = matmul_gelu(h, gmlp_w_in[0], gmlp_b_in[0])
    tril = jnp.tril(jnp.ones((CHUNK, CHUNK), dtype=bool))
    w_prompt = jnp.where(tril[None], gmlp_w_s[0], 0)
    seqs = CHUNK // len_s
    w_small = jnp.where(tril[None, :len_s, :len_s], gmlp_w_s[0][:, :len_s, :len_s], 0)
    w_sample = jnp.einsum("ab,gts->gatbs", jnp.eye(seqs, dtype=F32), w_small).reshape(N_SG, CHUNK, CHUNK)
    w_mix = jnp.stack([w_prompt, w_sample]).astype(BF16)
    bias_rows = jnp.stack([gmlp_b_s[0].T, jnp.tile(gmlp_b_s[0].T[:len_s], (seqs, 1))])
    bias_mix = jnp.repeat(bias_rows, SG_DIM, axis=2)
    y, v = gmlp_mix(z, w_mix, bias_mix, gmlp_ln_g[0], gmlp_ln_b[0], tp // CHUNK)
    x, h_t, route_t = matmul_residual(y, gmlp_w_out[0], gmlp_b_out[0], x, norm_ffn[1], moe_router[0])
    new_chunk_v_sample = v.reshape(nb_s, len_s, d)[None]

    e_slot = route_t[:TOP_K].T.astype(jnp.int32).reshape(-1)
    route = jnp.pad(route_t[:2 * TOP_K].T, ((0, 0), (0, LANES - 2 * TOP_K)))
    mblk1, ch1, rows_g = 1280, 320, 640
    n_blocks = (t * TOP_K) // mblk1 + N_EXP
    dest, src, blk_idx, blk_exp, blk_nch, gather_valid = _dispatch_plan(e_slot, N_EXP, mblk1, ch1, n_blocks, rows_g)
    xs = gather_rows(h_t, src, gather_valid, rows_g)
    eo = grouped_swiglu(xs, moe_w_gate[0], moe_w_up[0], moe_w_down[0], blk_idx, blk_exp, blk_nch, mblk1, ch1,
                        FFN_TF, FFN_TK, FFN_TN, True)
    out_p, out_s = combine_norm(x, route, dest, eo, norm_final, tp)
    return (out_p.reshape(nb_p, len_p, d), out_s.reshape(nb_s, len_s, d), new_conv_prompt, new_conv_sample,
            new_chunk_v_sample)
```
